```python
import jax, jax.numpy as jnp
from jax import lax
import numpy as np

D_MODEL = 2048
BATCH = 4
SEQ = 4096
DEPTH = 4
DEC_BATCH = 8
DEC_SEQ = 32
PAST_LEN = 2048

CHUNK = 64
CONV_WIDTH = 31
POOL_WINDOWS = (2, 4, 8, 16)
POOL_GROUPS = 4
POOL_GROUP_DIM = D_MODEL // POOL_GROUPS
POOL_MAX_WINDOW = 16
D_FF = 5632
N_EXPERTS = 8
TOP_K = 2
EXPERT_FF = 5632
N_CONV_LAYERS = (DEPTH + 1) // 2
N_POOL_LAYERS = DEPTH // 2
N_DENSE_LAYERS = (DEPTH + 1) // 2
N_MOE_LAYERS = DEPTH // 2
DEEPNORM_ALPHA = (2.0 * DEPTH) ** 0.25
DEEPNORM_BETA = (8.0 * DEPTH) ** -0.25
LN_EPS = 1e-5

kernel_name = 'hybrid_conformer_pool_moe_stream_step'


def _layernorm(x, g, b):
    xf = x.astype(jnp.float32)
    mu = jnp.mean(xf, axis=-1, keepdims=True)
    var = jnp.mean(jnp.square(xf - mu), axis=-1, keepdims=True)
    y = (xf - mu) * lax.rsqrt(var + LN_EPS)
    return (y * g.astype(jnp.float32) + b.astype(jnp.float32)).astype(x.dtype)


def _conv_module(x, state, w1, b1, dw, bdw, ln_g, ln_b, w2, b2):
    h = jnp.einsum('bld,de->ble', x, w1) + b1
    a, gate = jnp.split(h, 2, axis=-1)
    u = a * jax.nn.sigmoid(gate)
    up = jnp.concatenate([state.astype(u.dtype), u], axis=1)
    new_state = up[:, -(CONV_WIDTH - 1):]
    y = lax.conv_general_dilated(
        up, dw[:, None, :].astype(up.dtype), window_strides=(1,), padding='VALID',
        dimension_numbers=('NWC', 'WIO', 'NWC'), feature_group_count=up.shape[-1]) + bdw
    y = jax.nn.silu(_layernorm(y, ln_g, ln_b))
    return jnp.einsum('bld,de->ble', y, w2) + b2, new_state


def _pool_mixer(x, state, pos0, w, scale):
    b, l, d = x.shape
    p = POOL_MAX_WINDOW - 1
    xp = jnp.concatenate([state.astype(x.dtype), x], axis=1)
    new_state = xp[:, -p:]
    xf = xp.astype(jnp.float32)
    cs = jnp.concatenate([jnp.zeros((b, 1, d), jnp.float32), jnp.cumsum(xf, axis=1)], axis=1)
    pos = pos0 + jnp.arange(l, dtype=jnp.int32)
    cur = xf[:, p:]
    parts = []
    for g, win in enumerate(POOL_WINDOWS):
        lo, hi = g * POOL_GROUP_DIM, (g + 1) * POOL_GROUP_DIM
        s = cs[:, p + 1:p + 1 + l, lo:hi] - cs[:, p + 1 - win:p + 1 - win + l, lo:hi]
        cnt = jnp.minimum(pos + 1, win).astype(jnp.float32)[None, :, None]
        parts.append(s / cnt - cur[..., lo:hi])
    m = jnp.stack(parts, axis=2).astype(x.dtype)
    out = jnp.einsum('blgc,gce->blge', m, w).reshape(b, l, d)
    return out * scale, new_state


def _swiglu(x, w1, w3, w2):
    h = jax.nn.silu(jnp.einsum('bld,df->blf', x, w1)) * jnp.einsum('bld,df->blf', x, w3)
    return jnp.einsum('blf,fd->bld', h, w2)


def _moe(x, router, w1, w3, w2):
    logits = jnp.einsum('bld,de->ble', x, router).astype(jnp.float32)
    top_val, top_idx = lax.top_k(logits, TOP_K)
    gates = jax.nn.softmax(top_val, axis=-1)
    combine = jnp.sum(jax.nn.one_hot(top_idx, N_EXPERTS, dtype=jnp.float32) * gates[..., None],
                      axis=-2).astype(x.dtype)
    out = jnp.zeros_like(x)
    for e in range(N_EXPERTS):
        out = out + combine[..., e:e + 1] * _swiglu(x, w1[e], w3[e], w2[e])
    return out


def _trunk(x, conv_state, pool_state, pos0, p):
    new_conv, new_pool = [], []
    for i in range(DEPTH):
        j = i // 2
        if i % 2 == 0:
            h, s = _conv_module(x, conv_state[j], p['conv_w1'][j], p['conv_b1'][j], p['conv_dw'][j],
                                p['conv_bdw'][j], p['conv_ln_g'][j], p['conv_ln_b'][j],
                                p['conv_w2'][j], p['conv_b2'][j])
            new_conv.append(s)
        else:
            h, s = _pool_mixer(x, pool_state[j], pos0, p['pool_w'][j], p['pool_scale'][j])
            new_pool.append(s)
        x = _layernorm(DEEPNORM_ALPHA * x + h, p['ln_g'][i, 0], p['ln_b'][i, 0])
        if i % 2 == 0:
            f = _swiglu(x, p['ffn_w1'][j], p['ffn_w3'][j], p['ffn_w2'][j])
        else:
            f = _moe(x, p['moe_router'][j], p['moe_w1'][j], p['moe_w3'][j], p['moe_w2'][j])
        x = _layernorm(DEEPNORM_ALPHA * x + f, p['ln_g'][i, 1], p['ln_b'][i, 1])
    return x, jnp.stack(new_conv), jnp.stack(new_pool)


def setup_inputs(seed: int = 0) -> dict:
    key = jax.random.key(seed)
    ks = jax.random.split(key, 24)
    D = D_MODEL

    def nrm(k, shape, scale):
        return jax.random.normal(k, shape, jnp.float32) * scale

    return {
        'x_prompt': nrm(ks[0], (BATCH, SEQ, D), 1.0),
        'x_sample': nrm(ks[1], (DEC_BATCH, DEC_SEQ, D), 1.0),
        'cache_conv': nrm(ks[2], (N_CONV_LAYERS, DEC_BATCH, CONV_WIDTH - 1, D), 0.5),
        'cache_pool': nrm(ks[3], (N_POOL_LAYERS, DEC_BATCH, POOL_MAX_WINDOW - 1, D), 1.0),
        'conv_w1': nrm(ks[4], (N_CONV_LAYERS, D, 2 * D), D ** -0.5),
        'conv_b1': nrm(ks[5], (N_CONV_LAYERS, 2 * D), 0.01),
        'conv_dw': nrm(ks[6], (N_CONV_LAYERS, CONV_WIDTH, D), CONV_WIDTH ** -0.5),
        'conv_bdw': nrm(ks[7], (N_CONV_LAYERS, D), 0.01),
        'conv_ln_g': 1.0 + nrm(ks[8], (N_CONV_LAYERS, D), 0.02),
        'conv_ln_b': nrm(ks[9], (N_CONV_LAYERS, D), 0.01),
        'conv_w2': nrm(ks[10], (N_CONV_LAYERS, D, D), D ** -0.5 * DEEPNORM_BETA),
        'conv_b2': nrm(ks[11], (N_CONV_LAYERS, D), 0.01),
        'pool_w': nrm(ks[12], (N_POOL_LAYERS, POOL_GROUPS, POOL_GROUP_DIM, POOL_GROUP_DIM),
                      POOL_GROUP_DIM ** -0.5 * DEEPNORM_BETA),
        'pool_scale': 1.0 + nrm(ks[13], (N_POOL_LAYERS, D), 0.02),
        'ffn_w1': nrm(ks[14], (N_DENSE_LAYERS, D, D_FF), D ** -0.5),
        'ffn_w3': nrm(ks[15], (N_DENSE_LAYERS, D, D_FF), D ** -0.5),
        'ffn_w2': nrm(ks[16], (N_DENSE_LAYERS, D_FF, D), D_FF ** -0.5 * DEEPNORM_BETA),
        'moe_router': nrm(ks[17], (N_MOE_LAYERS, D, N_EXPERTS), D ** -0.5),
        'moe_w1': nrm(ks[18], (N_MOE_LAYERS, N_EXPERTS, D, EXPERT_FF), D ** -0.5),
        'moe_w3': nrm(ks[19], (N_MOE_LAYERS, N_EXPERTS, D, EXPERT_FF), D ** -0.5),
        'moe_w2': nrm(ks[20], (N_MOE_LAYERS, N_EXPERTS, EXPERT_FF, D), EXPERT_FF ** -0.5 * DEEPNORM_BETA),
        'ln_g': 1.0 + nrm(ks[21], (DEPTH, 2, D), 0.02),
        'ln_b': nrm(ks[22], (DEPTH, 2, D), 0.01),
    }


def reference(x_prompt, x_sample, cache_conv, cache_pool, conv_w1, conv_b1, conv_dw, conv_bdw,
              conv_ln_g, conv_ln_b, conv_w2, conv_b2, pool_w, pool_scale, ffn_w1, ffn_w3, ffn_w2,
              moe_router, moe_w1, moe_w3, moe_w2, ln_g, ln_b):
    p = {
        'conv_w1': conv_w1, 'conv_b1': conv_b1, 'conv_dw': conv_dw, 'conv_bdw': conv_bdw,
        'conv_ln_g': conv_ln_g, 'conv_ln_b': conv_ln_b, 'conv_w2': conv_w2, 'conv_b2': conv_b2,
        'pool_w': pool_w, 'pool_scale': pool_scale,
        'ffn_w1': ffn_w1, 'ffn_w3': ffn_w3, 'ffn_w2': ffn_w2,
        'moe_router': moe_router, 'moe_w1': moe_w1, 'moe_w3': moe_w3, 'moe_w2': moe_w2,
        'ln_g': ln_g, 'ln_b': ln_b,
    }
    b_p = x_prompt.shape[0]
    zero_conv = jnp.zeros((N_CONV_LAYERS, b_p, CONV_WIDTH - 1, D_MODEL), x_prompt.dtype)
    zero_pool = jnp.zeros((N_POOL_LAYERS, b_p, POOL_MAX_WINDOW - 1, D_MODEL), x_prompt.dtype)
    y_prompt, new_conv_prompt, new_pool_prompt = _trunk(x_prompt, zero_conv, zero_pool, 0, p)
    y_sample, new_conv_sample, new_pool_sample = _trunk(x_sample, cache_conv, cache_pool, PAST_LEN, p)
    return (y_prompt, y_sample, new_conv_prompt, new_pool_prompt, new_conv_sample, new_pool_sample)
```

```python
import functools

import jax
import jax.numpy as jnp
from jax import lax
from jax.experimental import pallas as pl
from jax.experimental.pallas import tpu as pltpu

F32 = jnp.float32
BF16 = jnp.bfloat16

PAST_LEN = 2048
LN_EPS = 1e-5
TOP_K = 2
LANES = 128
SUBLANES = 8
CONV_HALO = 32
POOL_HALO = 16
VMEM_LIMIT = 56 * 1024 * 1024


def _ln(x, g, b):
    mu = jnp.mean(x, axis=-1, keepdims=True)
    xc = x - mu
    var = jnp.mean(xc * xc, axis=-1, keepdims=True)
    return xc * lax.rsqrt(var + LN_EPS) * g + b


def _params(sem):
    return pltpu.CompilerParams(dimension_semantics=sem, vmem_limit_bytes=VMEM_LIMIT)


def _glu_kernel(x_ref, wa_ref, wg_ref, ba_ref, bg_ref, u_ref):
    xb = x_ref[...].astype(BF16)
    a = jnp.dot(xb, wa_ref[...], preferred_element_type=F32) + ba_ref[...]
    g = jnp.dot(xb, wg_ref[...], preferred_element_type=F32) + bg_ref[...]
    u_ref[...] = a * jax.nn.sigmoid(g)


def _glu(x, w1, b1, *, bm, bn):
    n, d = x.shape
    nj = d // bn
    b1 = b1.reshape(1, 2 * d)
    return pl.pallas_call(
        _glu_kernel,
        grid=(n // bm, nj),
        in_specs=[
            pl.BlockSpec((bm, d), lambda i, j: (i, 0)),
            pl.BlockSpec((d, bn), lambda i, j: (0, j)),
            pl.BlockSpec((d, bn), lambda i, j: (0, j + nj)),
            pl.BlockSpec((1, bn), lambda i, j: (0, j)),
            pl.BlockSpec((1, bn), lambda i, j: (0, j + nj)),
        ],
        out_specs=pl.BlockSpec((bm, bn), lambda i, j: (i, j)),
        out_shape=jax.ShapeDtypeStruct((n, d), F32),
        compiler_params=_params(("arbitrary", "arbitrary")),
        name="glu",
    )(x, w1, w1, b1, b1)


def _mixer_specs(n, d, bt, halo):
    tile = pl.BlockSpec((bt, d), lambda i: (i, 0))
    prev = pl.BlockSpec((halo, d), lambda i: (jnp.maximum(i * (bt // halo) - 1, 0), 0))
    vec = pl.BlockSpec((1, d), lambda i: (0, 0))
    return tile, prev, vec


def _conv_kernel(u_ref, halo_ref, state_ref, x_ref, dw_ref, bdw_ref, cg_ref, cb_ref, w2_ref, b2_ref,
                 g_ref, b_ref, o_ref, win_ref, y_ref, *, bt, ls, tiles_per_stream, prompt_tiles, alpha):
    d = o_ref.shape[-1]
    width = dw_ref.shape[0]
    off = CONV_HALO - (width - 1)
    i = pl.program_id(0)

    def conv(pieces):
        def lane_chunk(c, carry):
            c0 = pl.multiple_of(c * LANES, LANES)
            for base, rows, out in pieces:
                acc = jnp.zeros((rows, LANES), F32)
                for k in range(width):
                    lo = base + off + k
                    acc = acc + win_ref[lo:lo + rows, pl.ds(c0, LANES)] * dw_ref[k:k + 1, pl.ds(c0, LANES)]
                y_ref[out:out + rows, pl.ds(c0, LANES)] = acc
            return carry
        lax.fori_loop(0, d // LANES, lane_chunk, 0)

    @pl.when(i < prompt_tiles)
    def _():
        win_ref[0:CONV_HALO, :] = jnp.where(i % tiles_per_stream == 0, 0.0, halo_ref[...])
        win_ref[CONV_HALO:CONV_HALO + bt, :] = u_ref[...]
        conv([(0, bt, 0)])

    @pl.when(i >= prompt_tiles)
    def _():
        per = CONV_HALO + ls
        s0 = (i - prompt_tiles) * (bt // ls)
        for s in range(bt // ls):
            win_ref[s * per:s * per + CONV_HALO, :] = state_ref[s0 + s]
            win_ref[s * per + CONV_HALO:(s + 1) * per, :] = u_ref[s * ls:(s + 1) * ls, :]
        conv([(s * per, ls, s * ls) for s in range(bt // ls)])

    y = _ln(y_ref[...] + bdw_ref[...], cg_ref[...], cb_ref[...])
    y = y * jax.nn.sigmoid(y)
    h = jnp.dot(y.astype(BF16), w2_ref[...], preferred_element_type=F32) + b2_ref[...]
    o_ref[...] = _ln(alpha * x_ref[...] + h, g_ref[...], b_ref[...])


def _conv_mix(u, state, x, dw, bdw, cg, cb, w2, b2, g, b, *, bt, ls, lp, prompt_rows, alpha):
    n, d = u.shape
    tile, prev, vec = _mixer_specs(n, d, bt, CONV_HALO)
    win_rows = max(CONV_HALO + bt, (bt // ls) * (CONV_HALO + ls))
    return pl.pallas_call(
        functools.partial(_conv_kernel, bt=bt, ls=ls, tiles_per_stream=lp // bt,
                          prompt_tiles=prompt_rows // bt, alpha=alpha),
        grid=(n // bt,),
        in_specs=[tile, prev, pl.BlockSpec(state.shape, lambda i: (0, 0, 0)), tile,
                  pl.BlockSpec(dw.shape, lambda i: (0, 0)), vec, vec, vec,
                  pl.BlockSpec((d, d), lambda i: (0, 0)), vec, vec, vec],
        out_specs=tile,
        out_shape=jax.ShapeDtypeStruct((n, d), F32),
        scratch_shapes=[pltpu.VMEM((win_rows, d), F32), pltpu.VMEM((bt, d), F32)],
        compiler_params=_params(("arbitrary",)),
        name="conv_mix",
    )(u, u, state, x, dw, bdw, cg, cb, w2, b2, g, b)


def _pool_kernel(x_ref, halo_ref, state_ref, pw_ref, sc_ref, g_ref, b_ref, r_ref, o_ref, route_ref,
                 win_ref, m_ref, *, bt, ls, tiles_per_stream, prompt_tiles, n_experts, alpha):
    d = o_ref.shape[-1]
    groups = pw_ref.shape[0]
    gd = d // groups
    i = pl.program_id(0)

    def pool(base, rows, out, pos):
        for gi in range(groups):
            win = 2 ** (gi + 1)
            lo = gi * gd
            cur = win_ref[base + POOL_HALO:base + POOL_HALO + rows, lo:lo + gd]
            s = cur
            for k in range(1, win):
                s = s + win_ref[base + POOL_HALO - k:base + POOL_HALO - k + rows, lo:lo + gd]
            cnt = jnp.minimum(pos + 1, win).astype(F32)
            m_ref[out:out + rows, lo:lo + gd] = s / cnt - cur

    @pl.when(i < prompt_tiles)
    def _():
        t = i % tiles_per_stream
        win_ref[0:POOL_HALO, :] = jnp.where(t == 0, 0.0, halo_ref[...])
        win_ref[POOL_HALO:POOL_HALO + bt, :] = x_ref[...]
        pool(0, bt, 0, t * bt + lax.broadcasted_iota(jnp.int32, (bt, 1), 0))

    @pl.when(i >= prompt_tiles)
    def _():
        per = POOL_HALO + ls
        s0 = (i - prompt_tiles) * (bt // ls)
        pos = PAST_LEN + lax.broadcasted_iota(jnp.int32, (ls, 1), 0)
        for s in range(bt // ls):
            win_ref[s * per:s * per + POOL_HALO, :] = state_ref[s0 + s]
            win_ref[s * per + POOL_HALO:(s + 1) * per, :] = x_ref[s * ls:(s + 1) * ls, :]
            pool(s * per, ls, s * ls, pos)

    outs = [jnp.dot(m_ref[:, gi * gd:(gi + 1) * gd].astype(BF16), pw_ref[gi], preferred_element_type=F32)
            for gi in range(groups)]
    h = jnp.concatenate(outs, axis=-1) * sc_ref[...]
    x1 = _ln(alpha * x_ref[...] + h, g_ref[...], b_ref[...])
    o_ref[...] = x1

    logits = jnp.dot(x1, r_ref[...], preferred_element_type=F32, precision=lax.Precision.HIGHEST)
    lane = lax.broadcasted_iota(jnp.int32, logits.shape, 1).astype(F32)
    neg = -jnp.inf
    lg = jnp.where(lane < n_experts, logits, neg)
    v1 = jnp.max(lg, axis=-1, keepdims=True)
    i1 = jnp.min(jnp.where(lg == v1, lane, float(LANES)), axis=-1, keepdims=True)
    lg2 = jnp.where(lane == i1, neg, lg)
    v2 = jnp.max(lg2, axis=-1, keepdims=True)
    i2 = jnp.min(jnp.where(lg2 == v2, lane, float(LANES)), axis=-1, keepdims=True)
    e2 = jnp.exp(v2 - v1)
    den = 1.0 + e2
    route_ref[...] = jnp.where(lane == 0, i1, jnp.where(lane == 1, i2, jnp.where(
        lane == 2, 1.0 / den, jnp.where(lane == 3, e2 / den, 0.0))))


def _pool_mix(x, state, pw, sc, g, b, router, *, bt, ls, lp, prompt_rows, n_experts, alpha):
    n, d = x.shape
    tile, prev, vec = _mixer_specs(n, d, bt, POOL_HALO)
    win_rows = max(POOL_HALO + bt, (bt // ls) * (POOL_HALO + ls))
    return pl.pallas_call(
        functools.partial(_pool_kernel, bt=bt, ls=ls, tiles_per_stream=lp // bt,
                          prompt_tiles=prompt_rows // bt, n_experts=n_experts, alpha=alpha),
        grid=(n // bt,),
        in_specs=[tile, prev, pl.BlockSpec(state.shape, lambda i: (0, 0, 0)),
                  pl.BlockSpec(pw.shape, lambda i: (0, 0, 0)), vec, vec, vec,
                  pl.BlockSpec(router.shape, lambda i: (0, 0))],
        out_specs=[tile, pl.BlockSpec((bt, LANES), lambda i: (i, 0))],
        out_shape=[jax.ShapeDtypeStruct((n, d), F32), jax.ShapeDtypeStruct((n, LANES), F32)],
        scratch_shapes=[pltpu.VMEM((win_rows, d), F32), pltpu.VMEM((bt, d), F32)],
        compiler_params=_params(("arbitrary",)),
        name="pool_mix",
    )(x, x, state, pw, sc, g, b, router)


def _swiglu_step(xb, w1_ref, w3_ref, w2_ref, acc_ref):
    h1 = jnp.dot(xb, w1_ref[...], preferred_element_type=F32)
    h3 = jnp.dot(xb, w3_ref[...], preferred_element_type=F32)
    h = (h1 * jax.nn.sigmoid(h1) * h3).astype(BF16)
    acc_ref[...] += jnp.dot(h, w2_ref[...], preferred_element_type=F32)


def _ffn_kernel(x_ref, w1_ref, w3_ref, w2_ref, g_ref, b_ref, o_ref, xb_ref, acc_ref, *, alpha):
    f = pl.program_id(1)

    @pl.when(f == 0)
    def _():
        xb_ref[...] = x_ref[...].astype(BF16)
        acc_ref[...] = jnp.zeros_like(acc_ref)

    _swiglu_step(xb_ref[...], w1_ref, w3_ref, w2_ref, acc_ref)

    @pl.when(f == pl.num_programs(1) - 1)
    def _():
        o_ref[...] = _ln(alpha * x_ref[...] + acc_ref[...], g_ref[...], b_ref[...])


def _ffn(x, w1, w3, w2, g, b, *, bm, bf, alpha):
    n, d = x.shape
    ff = w1.shape[1]
    vec = pl.BlockSpec((1, d), lambda i, f: (0, 0))
    return pl.pallas_call(
        functools.partial(_ffn_kernel, alpha=alpha),
        grid=(n // bm, ff // bf),
        in_specs=[
            pl.BlockSpec((bm, d), lambda i, f: (i, 0)),
            pl.BlockSpec((d, bf), lambda i, f: (0, f)),
            pl.BlockSpec((d, bf), lambda i, f: (0, f)),
            pl.BlockSpec((bf, d), lambda i, f: (f, 0)),
            vec, vec,
        ],
        out_specs=pl.BlockSpec((bm, d), lambda i, f: (i, 0)),
        out_shape=jax.ShapeDtypeStruct((n, d), F32),
        scratch_shapes=[pltpu.VMEM((bm, d), BF16), pltpu.VMEM((bm, d), F32)],
        compiler_params=_params(("arbitrary", "arbitrary")),
        name="ffn_dense",
    )(x, w1, w3, w2, g, b)


def _moe_kernel(te_ref, nv_ref, rows_ref, src_ref, dst_ref, x_hbm, w1_ref, w3_ref, w2_ref, y_hbm,
                xf_ref, xb_ref, acc_ref, gsem, ssem, *, bm):
    t = pl.program_id(0)
    f = pl.program_id(1)
    valid = t < nv_ref[0]
    base = t * bm

    @pl.when(jnp.logical_and(valid, f == 0))
    def _():
        def issue(r, carry):
            tok = src_ref[base + r]
            pltpu.make_async_copy(x_hbm.at[pl.ds(tok, 1)], xf_ref.at[pl.ds(r, 1)], gsem).start()
            return carry
        lax.fori_loop(0, bm, issue, 0, unroll=8)
        pltpu.make_async_copy(x_hbm.at[pl.ds(0, bm)], xf_ref, gsem).wait()
        xb_ref[...] = xf_ref[...].astype(BF16)
        acc_ref[...] = jnp.zeros_like(acc_ref)

    @pl.when(valid)
    def _():
        _swiglu_step(xb_ref[...], w1_ref, w3_ref, w2_ref, acc_ref)

    @pl.when(jnp.logical_and(valid, f == pl.num_programs(1) - 1))
    def _():
        rows = rows_ref[t]

        def issue(r, carry):
            row = dst_ref[base + r]
            pltpu.make_async_copy(acc_ref.at[pl.ds(r, 1)], y_hbm.at[pl.ds(row, 1)], ssem).start()
            return carry
        lax.fori_loop(0, rows, issue, 0)
        bulk = pl.multiple_of((rows // SUBLANES) * SUBLANES, SUBLANES)

        @pl.when(bulk > 0)
        def _():
            pltpu.make_async_copy(acc_ref.at[pl.ds(0, bulk)], y_hbm.at[pl.ds(0, bulk)], ssem).wait()

        def wait_one(r, carry):
            pltpu.make_async_copy(acc_ref.at[pl.ds(0, 1)], y_hbm.at[pl.ds(0, 1)], ssem).wait()
            return carry
        lax.fori_loop(bulk, rows, wait_one, 0)


def _moe(x, plan, w1, w3, w2, *, bm, bf):
    n, d = x.shape
    ff = w1.shape[-1]
    nf = ff // bf
    n_tiles = plan[0].shape[0]

    def fidx(t, f, nv):
        return jnp.where(t < nv[0], f, nf - 1)

    grid_spec = pltpu.PrefetchScalarGridSpec(
        num_scalar_prefetch=5,
        grid=(n_tiles, nf),
        in_specs=[
            pl.BlockSpec(memory_space=pl.ANY),
            pl.BlockSpec((None, d, bf), lambda t, f, te, nv, *_: (te[t], 0, fidx(t, f, nv))),
            pl.BlockSpec((None, d, bf), lambda t, f, te, nv, *_: (te[t], 0, fidx(t, f, nv))),
            pl.BlockSpec((None, bf, d), lambda t, f, te, nv, *_: (te[t], fidx(t, f, nv), 0)),
        ],
        out_specs=pl.BlockSpec(memory_space=pl.ANY),
        scratch_shapes=[
            pltpu.VMEM((bm, d), F32), pltpu.VMEM((bm, d), BF16), pltpu.VMEM((bm, d), F32),
            pltpu.SemaphoreType.DMA(()), pltpu.SemaphoreType.DMA(()),
        ],
    )
    return pl.pallas_call(
        functools.partial(_moe_kernel, bm=bm),
        grid_spec=grid_spec,
        out_shape=jax.ShapeDtypeStruct((TOP_K * n, d), F32),
        compiler_params=_params(("arbitrary", "arbitrary")),
        name="moe_experts",
    )(*plan, x, w1, w3, w2)


def _route_plan(route, *, n_experts, bm, n_tiles):
    n = route.shape[0]
    e_flat = route[:, :TOP_K].astype(jnp.int32).reshape(-1)
    a = e_flat.shape[0]
    onehot = (e_flat[:, None] == jnp.arange(n_experts, dtype=jnp.int32)[None, :]).astype(jnp.int32)
    csum = jnp.cumsum(onehot, axis=0)
    rank = jnp.sum((csum - 1) * onehot, axis=1)
    counts = csum[-1]
    padded = ((counts + bm - 1) // bm) * bm
    pend = jnp.cumsum(padded)
    pstart = pend - padded
    q = jnp.sum(onehot * pstart[None, :], axis=1) + rank
    ar = jnp.arange(a, dtype=jnp.int32)
    tok = ar // TOP_K
    slot = ar % TOP_K
    p = n_tiles * bm
    src = jnp.zeros((p,), jnp.int32).at[q].set(tok)
    dst = jnp.zeros((p,), jnp.int32).at[q].set(slot * n + tok)
    n_valid = (pend[-1] // bm).astype(jnp.int32)
    tstart = jnp.arange(n_tiles, dtype=jnp.int32) * bm
    te_raw = jnp.minimum(jnp.sum((tstart[:, None] >= pend[None, :]).astype(jnp.int32), axis=1), n_experts - 1)
    te = te_raw[jnp.minimum(jnp.arange(n_tiles), n_valid - 1)].astype(jnp.int32)
    real_end = (pstart + counts)[te]
    rows = jnp.clip(real_end - tstart, 0, bm).astype(jnp.int32)
    return te, n_valid.reshape(1), rows, src, dst


def _combine_kernel(x_ref, y0_ref, y1_ref, route_ref, g_ref, b_ref, o_ref, *, alpha):
    r = route_ref[...]
    f = r[:, 2:3] * y0_ref[...] + r[:, 3:4] * y1_ref[...]
    o_ref[...] = _ln(alpha * x_ref[...] + f, g_ref[...], b_ref[...])


def _combine(x, y, route, g, b, *, bm, alpha):
    n, d = x.shape
    nb = n // bm
    vec = pl.BlockSpec((1, d), lambda i: (0, 0))
    return pl.pallas_call(
        functools.partial(_combine_kernel, alpha=alpha),
        grid=(nb,),
        in_specs=[
            pl.BlockSpec((bm, d), lambda i: (i, 0)),
            pl.BlockSpec((bm, d), lambda i: (i, 0)),
            pl.BlockSpec((bm, d), lambda i: (i + nb, 0)),
            pl.BlockSpec((bm, LANES), lambda i: (i, 0)),
            vec, vec,
        ],
        out_specs=pl.BlockSpec((bm, d), lambda i: (i, 0)),
        out_shape=jax.ShapeDtypeStruct((n, d), F32),
        compiler_params=_params(("arbitrary",)),
        name="moe_combine",
    )(x, y, y, route, g, b)


def _largest_divisor(n, cap, mult):
    return max(c for c in range(mult, cap + 1, mult) if n % c == 0)


def kernel(x_prompt, x_sample, cache_conv, cache_pool, conv_w1, conv_b1, conv_dw, conv_bdw, conv_ln_g, conv_ln_b, conv_w2, conv_b2, pool_w, pool_scale, ffn_w1, ffn_w3, ffn_w2, moe_router, moe_w1, moe_w3, moe_w2, ln_g, ln_b):
    bp, lp, d = x_prompt.shape
    bs, ls, _ = x_sample.shape
    depth = ln_g.shape[0]
    alpha = (2.0 * depth) ** 0.25
    n_experts = moe_router.shape[-1]
    width = conv_dw.shape[1]
    pmax = cache_pool.shape[2]
    np_, ns = bp * lp, bs * ls
    n = np_ + ns

    bt = 256
    assert lp % bt == 0 and ns % bt == 0 and bt % ls == 0 and bt % CONV_HALO == 0
    assert width - 1 <= CONV_HALO and pmax <= POOL_HALO and ls >= width - 1 and ls >= pmax
    bm_tok = _largest_divisor(n, 640, 8)
    bm_comb = _largest_divisor(n, 320, 8)
    bm_moe, bf = 512, 512
    n_tiles = -(-TOP_K * n // bm_moe) + n_experts

    x = jnp.concatenate([x_prompt.reshape(np_, d), x_sample.reshape(ns, d)], axis=0)
    row = lambda v: v.reshape(1, d)
    mix = dict(bt=bt, ls=ls, lp=lp, prompt_rows=np_, alpha=alpha)

    new_conv_p, new_conv_s, new_pool_p, new_pool_s = [], [], [], []
    for i in range(depth):
        j = i // 2
        if i % 2 == 0:
            u = _glu(x, conv_w1[j].astype(BF16), conv_b1[j], bm=bm_tok, bn=1024)
            state = jnp.pad(cache_conv[j], ((0, 0), (CONV_HALO - (width - 1), 0), (0, 0)))
            x1 = _conv_mix(u, state, x, conv_dw[j], row(conv_bdw[j]), row(conv_ln_g[j]), row(conv_ln_b[j]),
                           conv_w2[j].astype(BF16), row(conv_b2[j]), row(ln_g[i, 0]), row(ln_b[i, 0]), **mix)
            new_conv_p.append(u[:np_].reshape(bp, lp, d)[:, lp - (width - 1):])
            new_conv_s.append(u[np_:].reshape(bs, ls, d)[:, ls - (width - 1):])
            x = _ffn(x1, ffn_w1[j].astype(BF16), ffn_w3[j].astype(BF16), ffn_w2[j].astype(BF16),
                     row(ln_g[i, 1]), row(ln_b[i, 1]), bm=bm_tok, bf=bf, alpha=alpha)
        else:
            router = jnp.pad(moe_router[j], ((0, 0), (0, LANES - n_experts)))
            state = jnp.pad(cache_pool[j], ((0, 0), (POOL_HALO - pmax, 0), (0, 0)))
            new_pool_p.append(x[:np_].reshape(bp, lp, d)[:, lp - pmax:])
            new_pool_s.append(x[np_:].reshape(bs, ls, d)[:, ls - pmax:])
            x1, route = _pool_mix(x, state, pool_w[j].astype(BF16), row(pool_scale[j]), row(ln_g[i, 0]),
                                  row(ln_b[i, 0]), router, n_experts=n_experts, **mix)
            plan = _route_plan(route, n_experts=n_experts, bm=bm_moe, n_tiles=n_tiles)
            y = _moe(x1, plan, moe_w1[j].astype(BF16), moe_w3[j].astype(BF16), moe_w2[j].astype(BF16),
                     bm=bm_moe, bf=bf)
            x = _combine(x1, y, route, row(ln_g[i, 1]), row(ln_b[i, 1]), bm=bm_comb, alpha=alpha)

    y_prompt = x[:np_].reshape(bp, lp, d)
    y_sample = x[np_:].reshape(bs, ls, d)
    return (y_prompt, y_sample, jnp.stack(new_conv_p), jnp.stack(new_pool_p),
            jnp.stack(new_conv_s), jnp.stack(new_pool_s))
```

```python
import functools

import jax
import jax.numpy as jnp
from jax import lax
from jax.experimental import pallas as pl
from jax.experimental.pallas import tpu as pltpu

F32 = jnp.float32
BF16 = jnp.bfloat16

PAST_LEN = 2048
LN_EPS = 1e-5
TOP_K = 2
LANES = 128
SUBLANES = 8
CONV_PIECE = 128
CONV_HALO = 32
POOL_HALO = 16
VMEM_LIMIT = 56 * 1024 * 1024


def _ln(x, g, b):
    mu = jnp.mean(x, axis=-1, keepdims=True)
    xc = x - mu
    var = jnp.mean(xc * xc, axis=-1, keepdims=True)
    return xc * lax.rsqrt(var + LN_EPS) * g + b


def _params(sem):
    return pltpu.CompilerParams(dimension_semantics=sem, vmem_limit_bytes=VMEM_LIMIT)


def _glu_kernel(x_ref, wa_ref, wg_ref, ba_ref, bg_ref, u_ref):
    xb = x_ref[...].astype(BF16)
    a = jnp.dot(xb, wa_ref[...], preferred_element_type=F32) + ba_ref[...]
    g = jnp.dot(xb, wg_ref[...], preferred_element_type=F32) + bg_ref[...]
    u_ref[...] = a * jax.nn.sigmoid(g)


def _glu(x, w1, b1, *, layer, bm, bn):
    n, d = x.shape
    nj = d // bn
    b1 = b1.reshape(1, 2 * d)
    return pl.pallas_call(
        _glu_kernel,
        grid=(n // bm, nj),
        in_specs=[
            pl.BlockSpec((bm, d), lambda i, j: (i, 0)),
            pl.BlockSpec((None, d, bn), lambda i, j: (layer, 0, j)),
            pl.BlockSpec((None, d, bn), lambda i, j: (layer, 0, j + nj)),
            pl.BlockSpec((1, bn), lambda i, j: (0, j)),
            pl.BlockSpec((1, bn), lambda i, j: (0, j + nj)),
        ],
        out_specs=pl.BlockSpec((bm, bn), lambda i, j: (i, j)),
        out_shape=jax.ShapeDtypeStruct((n, d), F32),
        compiler_params=_params(("arbitrary", "arbitrary")),
        name="glu",
    )(x, w1, w1, b1, b1)


def _mixer_specs(n, d, bt, halo):
    tile = pl.BlockSpec((bt, d), lambda i: (i, 0))
    prev = pl.BlockSpec((halo, d), lambda i: (jnp.maximum(i * (bt // halo) - 1, 0), 0))
    vec = pl.BlockSpec((1, d), lambda i: (0, 0))
    return tile, prev, vec


def _conv_kernel(u_ref, halo_ref, state_ref, x_ref, dw_ref, bdw_ref, cg_ref, cb_ref, w2_ref, b2_ref,
                 g_ref, b_ref, o_ref, win_ref, y_ref, *, bt, ls, tiles_per_stream, prompt_tiles, alpha):
    d = o_ref.shape[-1]
    width = dw_ref.shape[0]
    off = CONV_HALO - (width - 1)
    i = pl.program_id(0)

    def conv(pieces):
        def lane_chunk(c, carry):
            c0 = pl.multiple_of(c * LANES, LANES)
            for base, rows, out in pieces:
                span = rows + CONV_HALO
                a_win = win_ref[base:base + span, pl.ds(c0, LANES)]
                acc = jnp.zeros((rows, LANES), F32)
                for b in range(SUBLANES):
                    taps = [k for k in range(width) if (off + k) % SUBLANES == b]
                    if not taps:
                        continue
                    sb = a_win if b == 0 else pltpu.roll(a_win, span - b, axis=0)
                    for k in taps:
                        lo = (off + k) - b
                        acc = acc + sb[lo:lo + rows] * dw_ref[k:k + 1, pl.ds(c0, LANES)]
                y_ref[out:out + rows, pl.ds(c0, LANES)] = acc
            return carry
        lax.fori_loop(0, d // LANES, lane_chunk, 0)

    @pl.when(i < prompt_tiles)
    def _():
        win_ref[0:CONV_HALO, :] = jnp.where(i % tiles_per_stream == 0, 0.0, halo_ref[...])
        win_ref[CONV_HALO:CONV_HALO + bt, :] = u_ref[...]
        conv([(r0, CONV_PIECE, r0) for r0 in range(0, bt, CONV_PIECE)])

    @pl.when(i >= prompt_tiles)
    def _():
        per = CONV_HALO + ls
        s0 = (i - prompt_tiles) * (bt // ls)
        for s in range(bt // ls):
            win_ref[s * per:s * per + CONV_HALO, :] = state_ref[s0 + s]
            win_ref[s * per + CONV_HALO:(s + 1) * per, :] = u_ref[s * ls:(s + 1) * ls, :]
        conv([(s * per, ls, s * ls) for s in range(bt // ls)])

    y = _ln(y_ref[...] + bdw_ref[...], cg_ref[...], cb_ref[...])
    y = y * jax.nn.sigmoid(y)
    h = jnp.dot(y.astype(BF16), w2_ref[...], preferred_element_type=F32) + b2_ref[...]
    o_ref[...] = _ln(alpha * x_ref[...] + h, g_ref[...], b_ref[...])


def _conv_mix(u, state, x, dw, bdw, cg, cb, w2, b2, g, b, *, layer, bt, ls, lp, prompt_rows, alpha):
    n, d = u.shape
    tile, prev, vec = _mixer_specs(n, d, bt, CONV_HALO)
    win_rows = max(CONV_HALO + bt, (bt // ls) * (CONV_HALO + ls))
    return pl.pallas_call(
        functools.partial(_conv_kernel, bt=bt, ls=ls, tiles_per_stream=lp // bt,
                          prompt_tiles=prompt_rows // bt, alpha=alpha),
        grid=(n // bt,),
        in_specs=[tile, prev, pl.BlockSpec(state.shape, lambda i: (0, 0, 0)), tile,
                  pl.BlockSpec(dw.shape, lambda i: (0, 0)), vec, vec, vec,
                  pl.BlockSpec((None, d, d), lambda i: (layer, 0, 0)), vec, vec, vec],
        out_specs=tile,
        out_shape=jax.ShapeDtypeStruct((n, d), F32),
        scratch_shapes=[pltpu.VMEM((win_rows, d), F32), pltpu.VMEM((bt, d), F32)],
        compiler_params=_params(("arbitrary",)),
        name="conv_mix",
    )(u, u, state, x, dw, bdw, cg, cb, w2, b2, g, b)


def _pool_kernel(x_ref, halo_ref, state_ref, pw_ref, sc_ref, g_ref, b_ref, r_ref, o_ref, route_ref,
                 win_ref, m_ref, *, bt, ls, tiles_per_stream, prompt_tiles, n_experts, alpha):
    d = o_ref.shape[-1]
    groups = pw_ref.shape[0]
    gd = d // groups
    i = pl.program_id(0)

    def pool(base, rows, out, pos):
        for gi in range(groups):
            win = 2 ** (gi + 1)
            lo = gi * gd
            cur = win_ref[base + POOL_HALO:base + POOL_HALO + rows, lo:lo + gd]
            s = cur
            for k in range(1, win):
                s = s + win_ref[base + POOL_HALO - k:base + POOL_HALO - k + rows, lo:lo + gd]
            cnt = jnp.minimum(pos + 1, win).astype(F32)
            m_ref[out:out + rows, lo:lo + gd] = s / cnt - cur

    @pl.when(i < prompt_tiles)
    def _():
        t = i % tiles_per_stream
        win_ref[0:POOL_HALO, :] = jnp.where(t == 0, 0.0, halo_ref[...])
        win_ref[POOL_HALO:POOL_HALO + bt, :] = x_ref[...]
        pool(0, bt, 0, t * bt + lax.broadcasted_iota(jnp.int32, (bt, 1), 0))

    @pl.when(i >= prompt_tiles)
    def _():
        per = POOL_HALO + ls
        s0 = (i - prompt_tiles) * (bt // ls)
        pos = PAST_LEN + lax.broadcasted_iota(jnp.int32, (ls, 1), 0)
        for s in range(bt // ls):
            win_ref[s * per:s * per + POOL_HALO, :] = state_ref[s0 + s]
            win_ref[s * per + POOL_HALO:(s + 1) * per, :] = x_ref[s * ls:(s + 1) * ls, :]
            pool(s * per, ls, s * ls, pos)

    outs = [jnp.dot(m_ref[:, gi * gd:(gi + 1) * gd].astype(BF16), pw_ref[gi], preferred_element_type=F32)
            for gi in range(groups)]
    h = jnp.concatenate(outs, axis=-1) * sc_ref[...]
    x1 = _ln(alpha * x_ref[...] + h, g_ref[...], b_ref[...])
    o_ref[...] = x1

    def halves(v):
        hi = v.astype(BF16)
        return hi, (v - hi.astype(F32)).astype(BF16)
    xh, xl = halves(x1)
    rh, rl = halves(r_ref[...])
    logits = (jnp.dot(xh, rh, preferred_element_type=F32) + jnp.dot(xl, rh, preferred_element_type=F32)
              + jnp.dot(xh, rl, preferred_element_type=F32))
    lane = lax.broadcasted_iota(jnp.int32, logits.shape, 1).astype(F32)
    neg = -jnp.inf
    lg = jnp.where(lane < n_experts, logits, neg)
    v1 = jnp.max(lg, axis=-1, keepdims=True)
    i1 = jnp.min(jnp.where(lg == v1, lane, float(LANES)), axis=-1, keepdims=True)
    lg2 = jnp.where(lane == i1, neg, lg)
    v2 = jnp.max(lg2, axis=-1, keepdims=True)
    i2 = jnp.min(jnp.where(lg2 == v2, lane, float(LANES)), axis=-1, keepdims=True)
    e2 = jnp.exp(v2 - v1)
    den = 1.0 + e2
    route_ref[...] = jnp.where(lane == 0, i1, jnp.where(lane == 1, i2, jnp.where(
        lane == 2, 1.0 / den, jnp.where(lane == 3, e2 / den, 0.0))))


def _pool_mix(x, state, pw, sc, g, b, router, *, layer, bt, ls, lp, prompt_rows, n_experts, alpha):
    n, d = x.shape
    tile, prev, vec = _mixer_specs(n, d, bt, POOL_HALO)
    win_rows = max(POOL_HALO + bt, (bt // ls) * (POOL_HALO + ls))
    return pl.pallas_call(
        functools.partial(_pool_kernel, bt=bt, ls=ls, tiles_per_stream=lp // bt,
                          prompt_tiles=prompt_rows // bt, n_experts=n_experts, alpha=alpha),
        grid=(n // bt,),
        in_specs=[tile, prev, pl.BlockSpec(state.shape, lambda i: (0, 0, 0)),
                  pl.BlockSpec((None,) + pw.shape[1:], lambda i: (layer, 0, 0, 0)), vec, vec, vec,
                  pl.BlockSpec(router.shape, lambda i: (0, 0))],
        out_specs=[tile, pl.BlockSpec((bt, LANES), lambda i: (i, 0))],
        out_shape=[jax.ShapeDtypeStruct((n, d), F32), jax.ShapeDtypeStruct((n, LANES), F32)],
        scratch_shapes=[pltpu.VMEM((win_rows, d), F32), pltpu.VMEM((bt, d), F32)],
        compiler_params=_params(("arbitrary",)),
        name="pool_mix",
    )(x, x, state, pw, sc, g, b, router)


def _ffn_kernel(x_ref, w1_ref, w3_ref, w2_ref, g_ref, b_ref, o_ref, xb_ref, acc_ref, *, alpha):
    f = pl.program_id(1)

    @pl.when(f == 0)
    def _():
        xb_ref[...] = x_ref[...].astype(BF16)
        acc_ref[...] = jnp.zeros_like(acc_ref)

    xb = xb_ref[...]
    h1 = jnp.dot(xb, w1_ref[...], preferred_element_type=F32)
    h3 = jnp.dot(xb, w3_ref[...], preferred_element_type=F32)
    h = (h1 * jax.nn.sigmoid(h1) * h3).astype(BF16)
    acc_ref[...] += jnp.dot(h, w2_ref[...], preferred_element_type=F32)

    @pl.when(f == pl.num_programs(1) - 1)
    def _():
        o_ref[...] = _ln(alpha * x_ref[...] + acc_ref[...], g_ref[...], b_ref[...])


def _ffn(x, w1, w3, w2, g, b, *, layer, bm, bf, alpha):
    n, d = x.shape
    ff = w1.shape[-1]
    vec = pl.BlockSpec((1, d), lambda i, f: (0, 0))
    return pl.pallas_call(
        functools.partial(_ffn_kernel, alpha=alpha),
        grid=(n // bm, ff // bf),
        in_specs=[
            pl.BlockSpec((bm, d), lambda i, f: (i, 0)),
            pl.BlockSpec((None, d, bf), lambda i, f: (layer, 0, f)),
            pl.BlockSpec((None, d, bf), lambda i, f: (layer, 0, f)),
            pl.BlockSpec((None, bf, d), lambda i, f: (layer, f, 0)),
            vec, vec,
        ],
        out_specs=pl.BlockSpec((bm, d), lambda i, f: (i, 0)),
        out_shape=jax.ShapeDtypeStruct((n, d), F32),
        scratch_shapes=[pltpu.VMEM((bm, d), BF16), pltpu.VMEM((bm, d), F32)],
        compiler_params=_params(("arbitrary", "arbitrary")),
        name="ffn_dense",
    )(x, w1, w3, w2, g, b)


def _moe_kernel(vt_ref, ve_ref, vlo_ref, vhi_ref, nv_ref, src_ref, dst_ref,
                x_hbm, w1_ref, w3_ref, w2_ref, y_hbm,
                xf_ref, xb_ref, acc_ref, yb_ref, gsem, ssem, *, bm, n_tiles, chunk):
    v = pl.program_id(0)
    f = pl.program_id(1)
    nf = pl.num_programs(1)
    nv = nv_ref[0]
    valid = v < nv
    tile = vt_ref[v]
    lo = vlo_ref[v]
    hi = vhi_ref[v]
    first = lo == 0
    last = hi == bm

    def gather_row(t, r):
        tok = src_ref[t * bm + r]
        return pltpu.make_async_copy(x_hbm.at[pl.ds(tok, 1)], xf_ref.at[pl.ds(r, 1)], gsem)

    def scatter_row(t, r):
        row = dst_ref[t * bm + r]
        return pltpu.make_async_copy(yb_ref.at[pl.ds(r, 1)], y_hbm.at[pl.ds(row, 1)], ssem)

    def gather_tile_wait():
        pltpu.make_async_copy(x_hbm.at[pl.ds(0, bm)], xf_ref, gsem).wait()

    def scatter_tile_wait():
        pltpu.make_async_copy(yb_ref, y_hbm.at[pl.ds(0, bm)], ssem).wait()

    @pl.when(jnp.logical_and(valid, jnp.logical_and(first, f == 0)))
    def _():
        @pl.when(v == 0)
        def _():
            def issue(r, carry):
                gather_row(0, r).start()
                return carry
            lax.fori_loop(0, bm, issue, 0, unroll=8)
        gather_tile_wait()
        xb_ref[...] = xf_ref[...].astype(BF16)
        acc_ref[...] = jnp.zeros_like(acc_ref)

    do_gather = jnp.logical_and(last, tile + 1 < n_tiles)
    do_scatter = jnp.logical_and(first, tile > 0)

    @pl.when(valid)
    def _():
        for i in range(chunk):
            r = f * chunk + i
            in_tile = r < bm
            rc = jnp.minimum(r, bm - 1)

            @pl.when(jnp.logical_and(do_gather, in_tile))
            def _():
                gather_row(jnp.minimum(tile + 1, n_tiles - 1), rc).start()

            @pl.when(jnp.logical_and(do_scatter, in_tile))
            def _():
                scatter_row(jnp.maximum(tile - 1, 0), rc).start()

        xb = xb_ref[...]
        h1 = jnp.dot(xb, w1_ref[...], preferred_element_type=F32)
        h3 = jnp.dot(xb, w3_ref[...], preferred_element_type=F32)
        rid = lax.broadcasted_iota(jnp.int32, (bm, 1), 0)
        mine = jnp.logical_and(rid >= lo, rid < hi)
        h = jnp.where(mine, h1 * jax.nn.sigmoid(h1) * h3, 0.0).astype(BF16)
        acc_ref[...] += jnp.dot(h, w2_ref[...], preferred_element_type=F32)

    @pl.when(jnp.logical_and(valid, f == nf - 1))
    def _():
        @pl.when(do_scatter)
        def _():
            scatter_tile_wait()

        @pl.when(last)
        def _():
            yb_ref[...] = acc_ref[...]

        @pl.when(v == nv - 1)
        def _():
            def issue(r, carry):
                scatter_row(tile, r).start()
                return carry
            lax.fori_loop(0, bm, issue, 0, unroll=8)
            scatter_tile_wait()


def _moe(x, plan, w1, w3, w2, *, layer, bm, bf):
    n, d = x.shape
    ff = w1.shape[-1]
    nf = ff // bf
    n_visits = plan[0].shape[0]
    n_tiles = TOP_K * n // bm
    chunk = -(-bm // (nf - 1))

    def fidx(v, f, nv):
        return jnp.where(v < nv[0], f, nf - 1)

    up = lambda v, f, vt, ve, vlo, vhi, nv, *_: (layer, ve[v], 0, fidx(v, f, nv))
    down = lambda v, f, vt, ve, vlo, vhi, nv, *_: (layer, ve[v], fidx(v, f, nv), 0)
    grid_spec = pltpu.PrefetchScalarGridSpec(
        num_scalar_prefetch=7,
        grid=(n_visits, nf),
        in_specs=[
            pl.BlockSpec(memory_space=pl.ANY),
            pl.BlockSpec((None, None, d, bf), up),
            pl.BlockSpec((None, None, d, bf), up),
            pl.BlockSpec((None, None, bf, d), down),
        ],
        out_specs=pl.BlockSpec(memory_space=pl.ANY),
        scratch_shapes=[
            pltpu.VMEM((bm, d), F32), pltpu.VMEM((bm, d), BF16), pltpu.VMEM((bm, d), F32),
            pltpu.VMEM((bm, d), F32), pltpu.SemaphoreType.DMA(()), pltpu.SemaphoreType.DMA(()),
        ],
    )
    return pl.pallas_call(
        functools.partial(_moe_kernel, bm=bm, n_tiles=n_tiles, chunk=chunk),
        grid_spec=grid_spec,
        out_shape=jax.ShapeDtypeStruct((TOP_K * n, d), F32),
        compiler_params=_params(("arbitrary", "arbitrary")),
        name="moe_experts",
    )(*plan, x, w1, w3, w2)


def _route_plan(route, *, n_experts, bm):
    n = route.shape[0]
    e_flat = route[:, :TOP_K].astype(jnp.int32).reshape(-1)
    a = e_flat.shape[0]
    n_tiles = a // bm
    n_visits = n_tiles + n_experts - 1
    asg = jnp.sort(e_flat * a + jnp.arange(a, dtype=jnp.int32)) % a
    src = asg // TOP_K
    dst = (asg % TOP_K) * n + src
    counts = jnp.sum((e_flat[:, None] == jnp.arange(n_experts, dtype=jnp.int32)[None, :]).astype(jnp.int32), axis=0)
    gend = jnp.cumsum(counts)
    cuts = jnp.sort(jnp.concatenate([jnp.arange(n_tiles, dtype=jnp.int32) * bm, gend]))
    lo, hi = cuts[:-1], cuts[1:]
    empty = hi <= lo
    order = jnp.argsort(empty, stable=True)
    nv = jnp.sum(jnp.logical_not(empty)).astype(jnp.int32)
    keep = order[jnp.minimum(jnp.arange(n_visits), nv - 1)]
    lo, hi = lo[keep], hi[keep]
    tile = lo // bm
    expert = jnp.sum((gend[None, :] <= lo[:, None]).astype(jnp.int32), axis=1)
    i32 = lambda t: t.astype(jnp.int32)
    return i32(tile), i32(expert), i32(lo - tile * bm), i32(hi - tile * bm), nv.reshape(1), i32(src), i32(dst)


def _combine_kernel(x_ref, y0_ref, y1_ref, route_ref, g_ref, b_ref, o_ref, *, alpha):
    r = route_ref[...]
    f = r[:, 2:3] * y0_ref[...] + r[:, 3:4] * y1_ref[...]
    o_ref[...] = _ln(alpha * x_ref[...] + f, g_ref[...], b_ref[...])


def _combine(x, y, route, g, b, *, bm, alpha):
    n, d = x.shape
    nb = n // bm
    vec = pl.BlockSpec((1, d), lambda i: (0, 0))
    return pl.pallas_call(
        functools.partial(_combine_kernel, alpha=alpha),
        grid=(nb,),
        in_specs=[
            pl.BlockSpec((bm, d), lambda i: (i, 0)),
            pl.BlockSpec((bm, d), lambda i: (i, 0)),
            pl.BlockSpec((bm, d), lambda i: (i + nb, 0)),
            pl.BlockSpec((bm, LANES), lambda i: (i, 0)),
            vec, vec,
        ],
        out_specs=pl.BlockSpec((bm, d), lambda i: (i, 0)),
        out_shape=jax.ShapeDtypeStruct((n, d), F32),
        compiler_params=_params(("arbitrary",)),
        name="moe_combine",
    )(x, y, y, route, g, b)


def _largest_divisor(n, cap, mult):
    return max(c for c in range(mult, cap + 1, mult) if n % c == 0)


def _tail_rows(buf, streams, length, first_row, keep):
    return jnp.stack([lax.slice_in_dim(buf, first_row + (s + 1) * length - keep, first_row + (s + 1) * length)
                      for s in range(streams)])


def kernel(x_prompt, x_sample, cache_conv, cache_pool, conv_w1, conv_b1, conv_dw, conv_bdw, conv_ln_g, conv_ln_b, conv_w2, conv_b2, pool_w, pool_scale, ffn_w1, ffn_w3, ffn_w2, moe_router, moe_w1, moe_w3, moe_w2, ln_g, ln_b):
    bp, lp, d = x_prompt.shape
    bs, ls, _ = x_sample.shape
    depth = ln_g.shape[0]
    alpha = (2.0 * depth) ** 0.25
    n_experts = moe_router.shape[-1]
    width = conv_dw.shape[1]
    pmax = cache_pool.shape[2]
    np_, ns = bp * lp, bs * ls
    n = np_ + ns

    bt = 256
    assert lp % bt == 0 and ns % bt == 0 and bt % ls == 0 and bt % CONV_HALO == 0 and bt % CONV_PIECE == 0
    assert width - 1 <= CONV_HALO and pmax <= POOL_HALO and ls >= width - 1 and ls >= pmax
    bm_tok = _largest_divisor(n, 640, 8)
    bm_comb = _largest_divisor(n, 320, 8)
    bm_moe, bf = 512, 512
    assert (TOP_K * n) % bm_moe == 0 and TOP_K * n * n_experts < 2 ** 31

    conv_w1, conv_w2, pool_w = conv_w1.astype(BF16), conv_w2.astype(BF16), pool_w.astype(BF16)
    ffn_w1, ffn_w3, ffn_w2 = ffn_w1.astype(BF16), ffn_w3.astype(BF16), ffn_w2.astype(BF16)
    moe_w1, moe_w3, moe_w2 = moe_w1.astype(BF16), moe_w3.astype(BF16), moe_w2.astype(BF16)

    x = jnp.concatenate([x_prompt.reshape(np_, d), x_sample.reshape(ns, d)], axis=0)
    row = lambda v: v.reshape(1, d)
    mix = dict(bt=bt, ls=ls, lp=lp, prompt_rows=np_, alpha=alpha)

    new_conv_p, new_conv_s, new_pool_p, new_pool_s = [], [], [], []
    for i in range(depth):
        j = i // 2
        if i % 2 == 0:
            u = _glu(x, conv_w1, conv_b1[j], layer=j, bm=bm_tok, bn=1024)
            state = jnp.pad(cache_conv[j], ((0, 0), (CONV_HALO - (width - 1), 0), (0, 0)))
            x1 = _conv_mix(u, state, x, conv_dw[j], row(conv_bdw[j]), row(conv_ln_g[j]), row(conv_ln_b[j]),
                           conv_w2, row(conv_b2[j]), row(ln_g[i, 0]), row(ln_b[i, 0]), layer=j, **mix)
            new_conv_p.append(_tail_rows(u, bp, lp, 0, width - 1))
            new_conv_s.append(_tail_rows(u, bs, ls, np_, width - 1))
            x = _ffn(x1, ffn_w1, ffn_w3, ffn_w2, row(ln_g[i, 1]), row(ln_b[i, 1]),
                     layer=j, bm=bm_tok, bf=bf, alpha=alpha)
        else:
            router = jnp.pad(moe_router[j], ((0, 0), (0, LANES - n_experts)))
            state = jnp.pad(cache_pool[j], ((0, 0), (POOL_HALO - pmax, 0), (0, 0)))
            new_pool_p.append(_tail_rows(x, bp, lp, 0, pmax))
            new_pool_s.append(_tail_rows(x, bs, ls, np_, pmax))
            x1, route = _pool_mix(x, state, pool_w, row(pool_scale[j]), row(ln_g[i, 0]), row(ln_b[i, 0]),
                                  router, layer=j, n_experts=n_experts, **mix)
            plan = _route_plan(route, n_experts=n_experts, bm=bm_moe)
            y = _moe(x1, plan, moe_w1, moe_w3, moe_w2, layer=j, bm=bm_moe, bf=bf)
            x = _combine(x1, y, route, row(ln_g[i, 1]), row(ln_b[i, 1]), bm=bm_comb, alpha=alpha)

    y_prompt = x[:np_].reshape(bp, lp, d)
    y_sample = x[np_:].reshape(bs, ls, d)
    return (y_prompt, y_sample, jnp.stack(new_conv_p), jnp.stack(new_pool_p),
            jnp.stack(new_conv_s), jnp.stack(new_pool_s))
```

```python
import functools

import jax
import jax.numpy as jnp
from jax import lax
from jax.experimental import pallas as pl
from jax.experimental.pallas import tpu as pltpu

F32 = jnp.float32
BF16 = jnp.bfloat16

PAST_LEN = 2048
LN_EPS = 1e-5
TOP_K = 2
LANES = 128
SUBLANES = 8
BF16_ROWS = 16
CONV_PIECE = 128
CONV_HALO = 32
POOL_HALO = 16
VMEM_LIMIT = 56 * 1024 * 1024


def _ln(x, g, b):
    mu = jnp.mean(x, axis=-1, keepdims=True)
    xc = x - mu
    var = jnp.mean(xc * xc, axis=-1, keepdims=True)
    return xc * lax.rsqrt(var + LN_EPS) * g + b


def _params(sem):
    return pltpu.CompilerParams(dimension_semantics=sem, vmem_limit_bytes=VMEM_LIMIT)


def _cast_io(jobs, step, n_steps):
    ins, outs, shapes = [], [], []
    for arr, first_row, rows, rb in jobs:
        nb = rows // rb
        assert rows % rb == 0 and first_row % rb == 0 and nb <= n_steps
        blk = lambda *idx, nb=nb: jnp.minimum(step(*idx), nb - 1)
        ins.append(pl.BlockSpec((rb, arr.shape[1]), lambda *idx, blk=blk, b0=first_row // rb: (b0 + blk(*idx), 0)))
        outs.append(pl.BlockSpec((rb, arr.shape[1]), lambda *idx, blk=blk: (blk(*idx), 0)))
        shapes.append(jax.ShapeDtypeStruct((rows, arr.shape[1]), BF16))
    return ins, outs, shapes


def _cast_blocks(cast_in, cast_out):
    for src, dst in zip(cast_in, cast_out):
        dst[...] = src[...].astype(BF16)


def _cast_rows(rows, n_steps):
    return min(c for c in range(BF16_ROWS, rows + 1, BF16_ROWS) if rows % c == 0 and rows // c <= n_steps)


def _glu_kernel(x_ref, wa_ref, wg_ref, ba_ref, bg_ref, u_ref):
    xb = x_ref[...].astype(BF16)
    a = jnp.dot(xb, wa_ref[...], preferred_element_type=F32) + ba_ref[...]
    g = jnp.dot(xb, wg_ref[...], preferred_element_type=F32) + bg_ref[...]
    u_ref[...] = a * jax.nn.sigmoid(g)


def _glu(x, w1, b1, *, layer, bm, bn):
    n, d = x.shape
    nj = d // bn
    b1 = b1.reshape(1, 2 * d)
    return pl.pallas_call(
        _glu_kernel,
        grid=(n // bm, nj),
        in_specs=[
            pl.BlockSpec((bm, d), lambda i, j: (i, 0)),
            pl.BlockSpec((None, d, bn), lambda i, j: (layer, 0, j)),
            pl.BlockSpec((None, d, bn), lambda i, j: (layer, 0, j + nj)),
            pl.BlockSpec((1, bn), lambda i, j: (0, j)),
            pl.BlockSpec((1, bn), lambda i, j: (0, j + nj)),
        ],
        out_specs=pl.BlockSpec((bm, bn), lambda i, j: (i, j)),
        out_shape=jax.ShapeDtypeStruct((n, d), F32),
        compiler_params=_params(("arbitrary", "arbitrary")),
        name="glu",
    )(x, w1, w1, b1, b1)


def _mixer_specs(n, d, bt, halo):
    tile = pl.BlockSpec((bt, d), lambda i: (i, 0))
    prev = pl.BlockSpec((halo, d), lambda i: (jnp.maximum(i * (bt // halo) - 1, 0), 0))
    vec = pl.BlockSpec((1, d), lambda i: (0, 0))
    return tile, prev, vec


def _conv_kernel(u_ref, halo_ref, state_ref, x_ref, dw_ref, bdw_ref, cg_ref, cb_ref, w2_ref, b2_ref,
                 g_ref, b_ref, o_ref, win_ref, y_ref, *, bt, ls, tiles_per_stream, prompt_tiles, alpha):
    d = o_ref.shape[-1]
    width = dw_ref.shape[0]
    off = CONV_HALO - (width - 1)
    i = pl.program_id(0)

    def conv(pieces):
        def lane_chunk(c, carry):
            c0 = pl.multiple_of(c * LANES, LANES)
            for base, rows, out in pieces:
                span = rows + CONV_HALO
                a_win = win_ref[base:base + span, pl.ds(c0, LANES)]
                acc = jnp.zeros((rows, LANES), F32)
                for b in range(SUBLANES):
                    taps = [k for k in range(width) if (off + k) % SUBLANES == b]
                    if not taps:
                        continue
                    sb = a_win if b == 0 else pltpu.roll(a_win, span - b, axis=0)
                    for k in taps:
                        lo = (off + k) - b
                        acc = acc + sb[lo:lo + rows] * dw_ref[k:k + 1, pl.ds(c0, LANES)]
                y_ref[out:out + rows, pl.ds(c0, LANES)] = acc
            return carry
        lax.fori_loop(0, d // LANES, lane_chunk, 0)

    @pl.when(i < prompt_tiles)
    def _():
        win_ref[0:CONV_HALO, :] = jnp.where(i % tiles_per_stream == 0, 0.0, halo_ref[...])
        win_ref[CONV_HALO:CONV_HALO + bt, :] = u_ref[...]
        conv([(r0, CONV_PIECE, r0) for r0 in range(0, bt, CONV_PIECE)])

    @pl.when(i >= prompt_tiles)
    def _():
        per = CONV_HALO + ls
        s0 = (i - prompt_tiles) * (bt // ls)
        for s in range(bt // ls):
            win_ref[s * per:s * per + CONV_HALO, :] = state_ref[s0 + s]
            win_ref[s * per + CONV_HALO:(s + 1) * per, :] = u_ref[s * ls:(s + 1) * ls, :]
        conv([(s * per, ls, s * ls) for s in range(bt // ls)])

    y = _ln(y_ref[...] + bdw_ref[...], cg_ref[...], cb_ref[...])
    y = y * jax.nn.sigmoid(y)
    h = jnp.dot(y.astype(BF16), w2_ref[...], preferred_element_type=F32) + b2_ref[...]
    o_ref[...] = _ln(alpha * x_ref[...] + h, g_ref[...], b_ref[...])


def _conv_mix(u, state, x, dw, bdw, cg, cb, w2, b2, g, b, *, layer, bt, ls, lp, prompt_rows, alpha):
    n, d = u.shape
    tile, prev, vec = _mixer_specs(n, d, bt, CONV_HALO)
    win_rows = max(CONV_HALO + bt, (bt // ls) * (CONV_HALO + ls))
    return pl.pallas_call(
        functools.partial(_conv_kernel, bt=bt, ls=ls, tiles_per_stream=lp // bt,
                          prompt_tiles=prompt_rows // bt, alpha=alpha),
        grid=(n // bt,),
        in_specs=[tile, prev, pl.BlockSpec(state.shape, lambda i: (0, 0, 0)), tile,
                  pl.BlockSpec(dw.shape, lambda i: (0, 0)), vec, vec, vec,
                  pl.BlockSpec((None, d, d), lambda i: (layer, 0, 0)), vec, vec, vec],
        out_specs=tile,
        out_shape=jax.ShapeDtypeStruct((n, d), F32),
        scratch_shapes=[pltpu.VMEM((win_rows, d), F32), pltpu.VMEM((bt, d), F32)],
        compiler_params=_params(("arbitrary",)),
        name="conv_mix",
    )(u, u, state, x, dw, bdw, cg, cb, w2, b2, g, b)


def _pool_kernel(x_ref, halo_ref, state_ref, pw_ref, sc_ref, g_ref, b_ref, r_ref, *rest,
                 bt, ls, tiles_per_stream, prompt_tiles, n_experts, alpha, n_cast):
    cast_in, (o_ref, route_ref, *rest) = rest[:n_cast], rest[n_cast:]
    cast_out, (win_ref, m_ref) = rest[:n_cast], rest[n_cast:]
    d = o_ref.shape[-1]
    groups = pw_ref.shape[0]
    gd = d // groups
    i = pl.program_id(0)

    def pool(base, rows, out, pos):
        for gi in range(groups):
            win = 2 ** (gi + 1)
            lo = gi * gd
            cur = win_ref[base + POOL_HALO:base + POOL_HALO + rows, lo:lo + gd]
            s = cur
            for k in range(1, win):
                s = s + win_ref[base + POOL_HALO - k:base + POOL_HALO - k + rows, lo:lo + gd]
            cnt = jnp.minimum(pos + 1, win).astype(F32)
            m_ref[out:out + rows, lo:lo + gd] = s / cnt - cur

    @pl.when(i < prompt_tiles)
    def _():
        t = i % tiles_per_stream
        win_ref[0:POOL_HALO, :] = jnp.where(t == 0, 0.0, halo_ref[...])
        win_ref[POOL_HALO:POOL_HALO + bt, :] = x_ref[...]
        pool(0, bt, 0, t * bt + lax.broadcasted_iota(jnp.int32, (bt, 1), 0))

    @pl.when(i >= prompt_tiles)
    def _():
        per = POOL_HALO + ls
        s0 = (i - prompt_tiles) * (bt // ls)
        pos = PAST_LEN + lax.broadcasted_iota(jnp.int32, (ls, 1), 0)
        for s in range(bt // ls):
            win_ref[s * per:s * per + POOL_HALO, :] = state_ref[s0 + s]
            win_ref[s * per + POOL_HALO:(s + 1) * per, :] = x_ref[s * ls:(s + 1) * ls, :]
            pool(s * per, ls, s * ls, pos)

    outs = [jnp.dot(m_ref[:, gi * gd:(gi + 1) * gd].astype(BF16), pw_ref[gi], preferred_element_type=F32)
            for gi in range(groups)]
    h = jnp.concatenate(outs, axis=-1) * sc_ref[...]
    x1 = _ln(alpha * x_ref[...] + h, g_ref[...], b_ref[...])
    o_ref[...] = x1

    def halves(v):
        hi = v.astype(BF16)
        return hi, (v - hi.astype(F32)).astype(BF16)
    xh, xl = halves(x1)
    rh, rl = halves(r_ref[...])
    logits = (jnp.dot(xh, rh, preferred_element_type=F32) + jnp.dot(xl, rh, preferred_element_type=F32)
              + jnp.dot(xh, rl, preferred_element_type=F32))
    lane = lax.broadcasted_iota(jnp.int32, logits.shape, 1).astype(F32)
    neg = -jnp.inf
    lg = jnp.where(lane < n_experts, logits, neg)
    v1 = jnp.max(lg, axis=-1, keepdims=True)
    i1 = jnp.min(jnp.where(lg == v1, lane, float(LANES)), axis=-1, keepdims=True)
    lg2 = jnp.where(lane == i1, neg, lg)
    v2 = jnp.max(lg2, axis=-1, keepdims=True)
    i2 = jnp.min(jnp.where(lg2 == v2, lane, float(LANES)), axis=-1, keepdims=True)
    e2 = jnp.exp(v2 - v1)
    den = 1.0 + e2
    route_ref[...] = jnp.where(lane == 0, i1, jnp.where(lane == 1, i2, jnp.where(
        lane == 2, 1.0 / den, jnp.where(lane == 3, e2 / den, 0.0))))
    _cast_blocks(cast_in, cast_out)


def _pool_mix(x, state, pw, sc, g, b, router, cast=(), *, layer, bt, ls, lp, prompt_rows, n_experts, alpha):
    n, d = x.shape
    tile, prev, vec = _mixer_specs(n, d, bt, POOL_HALO)
    win_rows = max(POOL_HALO + bt, (bt // ls) * (POOL_HALO + ls))
    cast_in, cast_out, cast_shapes = _cast_io(cast, lambda i: i, n // bt)
    return pl.pallas_call(
        functools.partial(_pool_kernel, bt=bt, ls=ls, tiles_per_stream=lp // bt,
                          prompt_tiles=prompt_rows // bt, n_experts=n_experts, alpha=alpha, n_cast=len(cast)),
        grid=(n // bt,),
        in_specs=[tile, prev, pl.BlockSpec(state.shape, lambda i: (0, 0, 0)),
                  pl.BlockSpec((None,) + pw.shape[1:], lambda i: (layer, 0, 0, 0)), vec, vec, vec,
                  pl.BlockSpec(router.shape, lambda i: (0, 0)), *cast_in],
        out_specs=[tile, pl.BlockSpec((bt, LANES), lambda i: (i, 0)), *cast_out],
        out_shape=[jax.ShapeDtypeStruct((n, d), F32), jax.ShapeDtypeStruct((n, LANES), F32), *cast_shapes],
        scratch_shapes=[pltpu.VMEM((win_rows, d), F32), pltpu.VMEM((bt, d), F32)],
        compiler_params=_params(("arbitrary",)),
        name="pool_mix",
    )(x, x, state, pw, sc, g, b, router, *[c[0] for c in cast])


def _ffn_kernel(x_ref, w1_ref, w3_ref, w2_ref, g_ref, b_ref, *rest, alpha, n_cast):
    cast_in, (o_ref, *rest) = rest[:n_cast], rest[n_cast:]
    cast_out, (xb_ref,) = rest[:n_cast], rest[n_cast:]
    f = pl.program_id(1)

    @pl.when(f == 0)
    def _():
        xb_ref[...] = x_ref[...].astype(BF16)
        o_ref[...] = jnp.zeros_like(o_ref)

    xb = xb_ref[...]
    h1 = jnp.dot(xb, w1_ref[...], preferred_element_type=F32)
    h3 = jnp.dot(xb, w3_ref[...], preferred_element_type=F32)
    h = (h1 * jax.nn.sigmoid(h1) * h3).astype(BF16)
    o_ref[...] += jnp.dot(h, w2_ref[...], preferred_element_type=F32)
    _cast_blocks(cast_in, cast_out)

    @pl.when(f == pl.num_programs(1) - 1)
    def _():
        o_ref[...] = _ln(alpha * x_ref[...] + o_ref[...], g_ref[...], b_ref[...])


def _ffn(x, w1, w3, w2, g, b, cast=(), *, layer, bm, bf, alpha):
    n, d = x.shape
    ff = w1.shape[-1]
    nf = ff // bf
    vec = pl.BlockSpec((1, d), lambda i, f: (0, 0))
    cast_in, cast_out, cast_shapes = _cast_io(cast, lambda i, f: i * nf + f, (n // bm) * nf)
    return pl.pallas_call(
        functools.partial(_ffn_kernel, alpha=alpha, n_cast=len(cast)),
        grid=(n // bm, nf),
        in_specs=[
            pl.BlockSpec((bm, d), lambda i, f: (i, 0)),
            pl.BlockSpec((None, d, bf), lambda i, f: (layer, 0, f)),
            pl.BlockSpec((None, d, bf), lambda i, f: (layer, 0, f)),
            pl.BlockSpec((None, bf, d), lambda i, f: (layer, f, 0)),
            vec, vec, *cast_in,
        ],
        out_specs=[pl.BlockSpec((bm, d), lambda i, f: (i, 0)), *cast_out],
        out_shape=[jax.ShapeDtypeStruct((n, d), F32), *cast_shapes],
        scratch_shapes=[pltpu.VMEM((bm, d), BF16)],
        compiler_params=_params(("arbitrary", "arbitrary")),
        name="ffn_dense",
    )(x, w1, w3, w2, g, b, *[c[0] for c in cast])


def _moe_kernel(vt_ref, ve_ref, vlo_ref, vhi_ref, nv_ref, src_ref, dst_ref,
                x_hbm, w1_ref, w3_ref, w2_ref, *rest, bm, n_tiles, chunk, n_cast):
    cast_in, (y_hbm, *rest) = rest[:n_cast], rest[n_cast:]
    cast_out, (xf_ref, xb_ref, acc_ref, yb_ref, gsem, ssem) = rest[:n_cast], rest[n_cast:]

    cast_blocks = functools.partial(_cast_blocks, cast_in, cast_out)

    v = pl.program_id(0)
    f = pl.program_id(1)
    nf = pl.num_programs(1)
    nv = nv_ref[0]
    valid = v < nv
    tile = vt_ref[v]
    lo = vlo_ref[v]
    hi = vhi_ref[v]
    first = lo == 0
    last = hi == bm

    def gather_row(t, r):
        tok = src_ref[t * bm + r]
        return pltpu.make_async_copy(x_hbm.at[pl.ds(tok, 1)], xf_ref.at[pl.ds(r, 1)], gsem)

    def scatter_row(t, r):
        row = dst_ref[t * bm + r]
        return pltpu.make_async_copy(yb_ref.at[pl.ds(r, 1)], y_hbm.at[pl.ds(row, 1)], ssem)

    def gather_tile_wait():
        pltpu.make_async_copy(x_hbm.at[pl.ds(0, bm)], xf_ref, gsem).wait()

    def scatter_tile_wait():
        pltpu.make_async_copy(yb_ref, y_hbm.at[pl.ds(0, bm)], ssem).wait()

    @pl.when(jnp.logical_and(valid, jnp.logical_and(first, f == 0)))
    def _():
        @pl.when(v == 0)
        def _():
            def issue(r, carry):
                gather_row(0, r).start()
                return carry
            lax.fori_loop(0, bm, issue, 0, unroll=8)
        gather_tile_wait()
        xb_ref[...] = xf_ref[...].astype(BF16)
        acc_ref[...] = jnp.zeros_like(acc_ref)

    do_gather = jnp.logical_and(last, tile + 1 < n_tiles)
    do_scatter = jnp.logical_and(first, tile > 0)

    @pl.when(valid)
    def _():
        for i in range(chunk):
            r = f * chunk + i
            in_tile = r < bm
            rc = jnp.minimum(r, bm - 1)

            @pl.when(jnp.logical_and(do_gather, in_tile))
            def _():
                gather_row(jnp.minimum(tile + 1, n_tiles - 1), rc).start()

            @pl.when(jnp.logical_and(do_scatter, in_tile))
            def _():
                scatter_row(jnp.maximum(tile - 1, 0), rc).start()

        xb = xb_ref[...]
        h1 = jnp.dot(xb, w1_ref[...], preferred_element_type=F32)
        h3 = jnp.dot(xb, w3_ref[...], preferred_element_type=F32)
        rid = lax.broadcasted_iota(jnp.int32, (bm, 1), 0)
        mine = jnp.logical_and(rid >= lo, rid < hi)
        h = jnp.where(mine, h1 * jax.nn.sigmoid(h1) * h3, 0.0).astype(BF16)
        acc_ref[...] += jnp.dot(h, w2_ref[...], preferred_element_type=F32)
        cast_blocks()

    @pl.when(jnp.logical_not(valid))
    def _():
        cast_blocks()

    @pl.when(jnp.logical_and(valid, f == nf - 1))
    def _():
        @pl.when(do_scatter)
        def _():
            scatter_tile_wait()

        @pl.when(last)
        def _():
            yb_ref[...] = acc_ref[...]

        @pl.when(v == nv - 1)
        def _():
            def issue(r, carry):
                scatter_row(tile, r).start()
                return carry
            lax.fori_loop(0, bm, issue, 0, unroll=8)
            scatter_tile_wait()


def _moe(x, plan, w1, w3, w2, cast=(), *, bm, bf):
    n, d = x.shape
    ff = w1.shape[-1]
    nf = ff // bf
    n_visits = plan[0].shape[0]
    n_tiles = TOP_K * n // bm
    chunk = -(-bm // (nf - 1))

    def fidx(v, f, nv):
        return jnp.where(v < nv[0], f, nf - 1)

    up = lambda v, f, vt, ve, vlo, vhi, nv, *_: (ve[v], 0, fidx(v, f, nv))
    down = lambda v, f, vt, ve, vlo, vhi, nv, *_: (ve[v], fidx(v, f, nv), 0)
    cast_in, cast_out, cast_shapes = _cast_io(cast, lambda v, f, *_: v * nf + f, n_visits * nf)
    grid_spec = pltpu.PrefetchScalarGridSpec(
        num_scalar_prefetch=7,
        grid=(n_visits, nf),
        in_specs=[
            pl.BlockSpec(memory_space=pl.ANY),
            pl.BlockSpec((None, d, bf), up),
            pl.BlockSpec((None, d, bf), up),
            pl.BlockSpec((None, bf, d), down),
            *cast_in,
        ],
        out_specs=[pl.BlockSpec(memory_space=pl.ANY), *cast_out],
        scratch_shapes=[
            pltpu.VMEM((bm, d), F32), pltpu.VMEM((bm, d), BF16), pltpu.VMEM((bm, d), F32),
            pltpu.VMEM((bm, d), F32), pltpu.SemaphoreType.DMA(()), pltpu.SemaphoreType.DMA(()),
        ],
    )
    return pl.pallas_call(
        functools.partial(_moe_kernel, bm=bm, n_tiles=n_tiles, chunk=chunk, n_cast=len(cast)),
        grid_spec=grid_spec,
        out_shape=[jax.ShapeDtypeStruct((TOP_K * n, d), F32), *cast_shapes],
        compiler_params=_params(("arbitrary", "arbitrary")),
        name="moe_experts",
    )(*plan, x, w1, w3, w2, *[c[0] for c in cast])


def _route_plan(route, *, n_experts, bm):
    n = route.shape[0]
    e_flat = route[:, :TOP_K].astype(jnp.int32).reshape(-1)
    a = e_flat.shape[0]
    n_tiles = a // bm
    n_visits = n_tiles + n_experts - 1
    asg = jnp.sort(e_flat * a + jnp.arange(a, dtype=jnp.int32)) % a
    src = asg // TOP_K
    dst = (asg % TOP_K) * n + src
    counts = jnp.sum((e_flat[:, None] == jnp.arange(n_experts, dtype=jnp.int32)[None, :]).astype(jnp.int32), axis=0)
    gend = jnp.cumsum(counts)
    cuts = jnp.sort(jnp.concatenate([jnp.arange(n_tiles, dtype=jnp.int32) * bm, gend]))
    lo, hi = cuts[:-1], cuts[1:]
    empty = hi <= lo
    order = jnp.argsort(empty, stable=True)
    nv = jnp.sum(jnp.logical_not(empty)).astype(jnp.int32)
    keep = order[jnp.minimum(jnp.arange(n_visits), nv - 1)]
    lo, hi = lo[keep], hi[keep]
    tile = lo // bm
    expert = jnp.sum((gend[None, :] <= lo[:, None]).astype(jnp.int32), axis=1)
    i32 = lambda t: t.astype(jnp.int32)
    return i32(tile), i32(expert), i32(lo - tile * bm), i32(hi - tile * bm), nv.reshape(1), i32(src), i32(dst)


def _combine_kernel(x_ref, y0_ref, y1_ref, route_ref, g_ref, b_ref, *o_refs, alpha, head_blocks):
    r = route_ref[...]
    f = r[:, 2:3] * y0_ref[...] + r[:, 3:4] * y1_ref[...]
    res = _ln(alpha * x_ref[...] + f, g_ref[...], b_ref[...])
    if head_blocks is None:
        o_refs[0][...] = res
    else:
        @pl.when(pl.program_id(0) < head_blocks)
        def _():
            o_refs[0][...] = res

        @pl.when(pl.program_id(0) >= head_blocks)
        def _():
            o_refs[1][...] = res


def _combine(x, y, route, g, b, *, bm, alpha, split_rows=None):
    n, d = x.shape
    nb = n // bm
    vec = pl.BlockSpec((1, d), lambda i: (0, 0))
    if split_rows is None:
        head_blocks = None
        out_specs = pl.BlockSpec((bm, d), lambda i: (i, 0))
        out_shape = jax.ShapeDtypeStruct((n, d), F32)
    else:
        assert split_rows % bm == 0
        head_blocks = split_rows // bm
        out_specs = [pl.BlockSpec((bm, d), lambda i: (jnp.minimum(i, head_blocks - 1), 0)),
                     pl.BlockSpec((bm, d), lambda i: (jnp.maximum(i - head_blocks, 0), 0))]
        out_shape = [jax.ShapeDtypeStruct((split_rows, d), F32), jax.ShapeDtypeStruct((n - split_rows, d), F32)]
    return pl.pallas_call(
        functools.partial(_combine_kernel, alpha=alpha, head_blocks=head_blocks),
        grid=(nb,),
        in_specs=[
            pl.BlockSpec((bm, d), lambda i: (i, 0)),
            pl.BlockSpec((bm, d), lambda i: (i, 0)),
            pl.BlockSpec((bm, d), lambda i: (i + nb, 0)),
            pl.BlockSpec((bm, LANES), lambda i: (i, 0)),
            vec, vec,
        ],
        out_specs=out_specs,
        out_shape=out_shape,
        compiler_params=_params(("arbitrary",)),
        name="moe_combine",
    )(x, y, y, route, g, b)


def _largest_divisor(n, cap, mult):
    return max(c for c in range(mult, cap + 1, mult) if n % c == 0)


def _tail_rows(buf, streams, length, first_row, keep):
    return jnp.stack([lax.slice_in_dim(buf, first_row + (s + 1) * length - keep, first_row + (s + 1) * length)
                      for s in range(streams)])


def kernel(x_prompt, x_sample, cache_conv, cache_pool, conv_w1, conv_b1, conv_dw, conv_bdw, conv_ln_g, conv_ln_b, conv_w2, conv_b2, pool_w, pool_scale, ffn_w1, ffn_w3, ffn_w2, moe_router, moe_w1, moe_w3, moe_w2, ln_g, ln_b):
    bp, lp, d = x_prompt.shape
    bs, ls, _ = x_sample.shape
    depth = ln_g.shape[0]
    alpha = (2.0 * depth) ** 0.25
    n_experts = moe_router.shape[-1]
    width = conv_dw.shape[1]
    pmax = cache_pool.shape[2]
    np_, ns = bp * lp, bs * ls
    n = np_ + ns

    bt = 256
    assert lp % bt == 0 and ns % bt == 0 and bt % ls == 0 and bt % CONV_HALO == 0 and bt % CONV_PIECE == 0
    assert width - 1 <= CONV_HALO and pmax <= POOL_HALO and ls >= width - 1 and ls >= pmax
    bm_tok = _largest_divisor(n, 640, 8)
    bm_comb = bt
    bm_moe, bf = 512, 512
    assert (TOP_K * n) % bm_moe == 0 and TOP_K * n * n_experts < 2 ** 31

    conv_w1, conv_w2, pool_w = conv_w1.astype(BF16), conv_w2.astype(BF16), pool_w.astype(BF16)
    ffn_w1, ffn_w3, ffn_w2 = ffn_w1.astype(BF16), ffn_w3.astype(BF16), ffn_w2.astype(BF16)
    n_moe, _, _, ff = moe_w1.shape
    up_rows, down_rows = n_experts * d, n_experts * ff
    moe_f32 = (moe_w1.reshape(n_moe * up_rows, ff), moe_w3.reshape(n_moe * up_rows, ff),
               moe_w2.reshape(n_moe * down_rows, d))
    moe_bf16 = {}

    def cast_jobs(arrays, rows, layer, n_steps):
        return tuple((w, layer * r, r, _cast_rows(r, n_steps)) for w, r in zip(arrays, rows))

    def as_experts(up1, up3, down):
        return up1.reshape(n_experts, d, ff), up3.reshape(n_experts, d, ff), down.reshape(n_experts, ff, d)

    x = jnp.concatenate([x_prompt.reshape(np_, d), x_sample.reshape(ns, d)], axis=0)
    row = lambda v: v.reshape(1, d)
    mix = dict(bt=bt, ls=ls, lp=lp, prompt_rows=np_, alpha=alpha)

    new_conv_p, new_conv_s, new_pool_p, new_pool_s = [], [], [], []
    for i in range(depth):
        j = i // 2
        if i % 2 == 0:
            u = _glu(x, conv_w1, conv_b1[j], layer=j, bm=bm_tok, bn=1024)
            state = jnp.pad(cache_conv[j], ((0, 0), (CONV_HALO - (width - 1), 0), (0, 0)))
            first = i == 0 and n_moe > 0
            up1_job = cast_jobs(moe_f32[:1], (up_rows,), 0, (n // bm_tok) * (ff // bf)) if first else ()
            x1 = _conv_mix(u, state, x, conv_dw[j], row(conv_bdw[j]), row(conv_ln_g[j]), row(conv_ln_b[j]),
                           conv_w2, row(conv_b2[j]), row(ln_g[i, 0]), row(ln_b[i, 0]), layer=j, **mix)
            new_conv_p.append(_tail_rows(u, bp, lp, 0, width - 1))
            new_conv_s.append(_tail_rows(u, bs, ls, np_, width - 1))
            x, *first_up1 = _ffn(x1, ffn_w1, ffn_w3, ffn_w2, row(ln_g[i, 1]), row(ln_b[i, 1]), up1_job,
                                 layer=j, bm=bm_tok, bf=bf, alpha=alpha)
        else:
            router = jnp.pad(moe_router[j], ((0, 0), (0, LANES - n_experts)))
            state = jnp.pad(cache_pool[j], ((0, 0), (POOL_HALO - pmax, 0), (0, 0)))
            new_pool_p.append(_tail_rows(x, bp, lp, 0, pmax))
            new_pool_s.append(_tail_rows(x, bs, ls, np_, pmax))
            up3_job = cast_jobs(moe_f32[1:2], (up_rows,), 0, n // bt) if j == 0 else ()
            x1, route, *first_up3 = _pool_mix(x, state, pool_w, row(pool_scale[j]), row(ln_g[i, 0]),
                                              row(ln_b[i, 0]), router, up3_job, layer=j, n_experts=n_experts, **mix)
            if j == 0:
                moe_bf16[0] = as_experts(*first_up1, *first_up3, moe_w2[0].astype(BF16))
            plan = _route_plan(route, n_experts=n_experts, bm=bm_moe)
            moe_steps = (TOP_K * n // bm_moe + n_experts - 1) * (ff // bf)
            cast = cast_jobs(moe_f32, (up_rows, up_rows, down_rows), j + 1, moe_steps) if j + 1 < n_moe else ()
            y, *nxt = _moe(x1, plan, *moe_bf16[j], cast, bm=bm_moe, bf=bf)
            if nxt:
                moe_bf16[j + 1] = as_experts(*nxt)
            x = _combine(x1, y, route, row(ln_g[i, 1]), row(ln_b[i, 1]), bm=bm_comb, alpha=alpha,
                         split_rows=np_ if i == depth - 1 else None)

    y_prompt, y_sample = x if isinstance(x, (list, tuple)) else (x[:np_], x[np_:])
    return (y_prompt.reshape(bp, lp, d), y_sample.reshape(bs, ls, d), jnp.stack(new_conv_p),
            jnp.stack(new_pool_p), jnp.stack(new_conv_s), jnp.stack(new_pool_s))
```

```python
import functools

import jax
import jax.numpy as jnp
from jax import lax
from jax.experimental import pallas as pl
from jax.experimental.pallas import tpu as pltpu

F32 = jnp.float32
BF16 = jnp.bfloat16

PAST_LEN = 2048
LN_EPS = 1e-5
TOP_K = 2
LANES = 128
SUBLANES = 8
BF16_ROWS = 16
CONV_PIECE = 128
POOL_PIECE = 128
CONV_HALO = 32
POOL_HALO = 16
VMEM_LIMIT = 56 * 1024 * 1024


def _ln(x, g, b):
    mu = jnp.mean(x, axis=-1, keepdims=True)
    xc = x - mu
    var = jnp.mean(xc * xc, axis=-1, keepdims=True)
    return xc * lax.rsqrt(var + LN_EPS) * g + b


def _params(sem):
    return pltpu.CompilerParams(dimension_semantics=sem, vmem_limit_bytes=VMEM_LIMIT)


def _cast_io(jobs, step, n_steps):
    ins, outs, shapes = [], [], []
    for arr, first_row, rows, rb in jobs:
        nb = rows // rb
        assert rows % rb == 0 and first_row % rb == 0 and nb <= n_steps
        blk = lambda *idx, nb=nb: jnp.minimum(step(*idx), nb - 1)
        ins.append(pl.BlockSpec((rb, arr.shape[1]), lambda *idx, blk=blk, b0=first_row // rb: (b0 + blk(*idx), 0)))
        outs.append(pl.BlockSpec((rb, arr.shape[1]), lambda *idx, blk=blk: (blk(*idx), 0)))
        shapes.append(jax.ShapeDtypeStruct((rows, arr.shape[1]), BF16))
    return ins, outs, shapes


def _cast_blocks(cast_in, cast_out):
    for src, dst in zip(cast_in, cast_out):
        dst[...] = src[...].astype(BF16)


def _cast_rows(rows, n_steps):
    return min(c for c in range(BF16_ROWS, rows + 1, BF16_ROWS) if rows % c == 0 and rows // c <= n_steps)


def _glu_kernel(x_ref, wa_ref, wg_ref, ba_ref, bg_ref, u_ref):
    xb = x_ref[...].astype(BF16)
    a = jnp.dot(xb, wa_ref[...], preferred_element_type=F32) + ba_ref[...]
    g = jnp.dot(xb, wg_ref[...], preferred_element_type=F32) + bg_ref[...]
    u_ref[...] = a * jax.nn.sigmoid(g)


def _glu(x, w1, b1, *, layer, bm, bn):
    n, d = x.shape
    nj = d // bn
    b1 = b1.reshape(1, 2 * d)
    return pl.pallas_call(
        _glu_kernel,
        grid=(n // bm, nj),
        in_specs=[
            pl.BlockSpec((bm, d), lambda i, j: (i, 0)),
            pl.BlockSpec((None, d, bn), lambda i, j: (layer, 0, j)),
            pl.BlockSpec((None, d, bn), lambda i, j: (layer, 0, j + nj)),
            pl.BlockSpec((1, bn), lambda i, j: (0, j)),
            pl.BlockSpec((1, bn), lambda i, j: (0, j + nj)),
        ],
        out_specs=pl.BlockSpec((bm, bn), lambda i, j: (i, j)),
        out_shape=jax.ShapeDtypeStruct((n, d), F32),
        compiler_params=_params(("arbitrary", "arbitrary")),
        name="glu",
    )(x, w1, w1, b1, b1)


def _mixer_specs(n, d, bt, halo):
    tile = pl.BlockSpec((bt, d), lambda i: (i, 0))
    prev = pl.BlockSpec((halo, d), lambda i: (jnp.maximum(i * (bt // halo) - 1, 0), 0))
    vec = pl.BlockSpec((1, d), lambda i: (0, 0))
    return tile, prev, vec


def _conv_kernel(u_ref, halo_ref, state_ref, x_ref, dw_ref, bdw_ref, cg_ref, cb_ref, w2_ref, b2_ref,
                 g_ref, b_ref, o_ref, win_ref, y_ref, *, bt, ls, tiles_per_stream, prompt_tiles, alpha):
    d = o_ref.shape[-1]
    width = dw_ref.shape[0]
    off = CONV_HALO - (width - 1)
    i = pl.program_id(0)

    def conv(pieces):
        def lane_chunk(c, carry):
            c0 = pl.multiple_of(c * LANES, LANES)
            for base, rows, out in pieces:
                span = rows + CONV_HALO
                a_win = win_ref[base:base + span, pl.ds(c0, LANES)]
                acc = jnp.zeros((rows, LANES), F32)
                for b in range(SUBLANES):
                    taps = [k for k in range(width) if (off + k) % SUBLANES == b]
                    if not taps:
                        continue
                    sb = a_win if b == 0 else pltpu.roll(a_win, span - b, axis=0)
                    for k in taps:
                        lo = (off + k) - b
                        acc = acc + sb[lo:lo + rows] * dw_ref[k:k + 1, pl.ds(c0, LANES)]
                y_ref[out:out + rows, pl.ds(c0, LANES)] = acc
            return carry
        lax.fori_loop(0, d // LANES, lane_chunk, 0)

    @pl.when(i < prompt_tiles)
    def _():
        win_ref[0:CONV_HALO, :] = jnp.where(i % tiles_per_stream == 0, 0.0, halo_ref[...])
        win_ref[CONV_HALO:CONV_HALO + bt, :] = u_ref[...]
        conv([(r0, CONV_PIECE, r0) for r0 in range(0, bt, CONV_PIECE)])

    @pl.when(i >= prompt_tiles)
    def _():
        per = CONV_HALO + ls
        s0 = (i - prompt_tiles) * (bt // ls)
        for s in range(bt // ls):
            win_ref[s * per:s * per + CONV_HALO, :] = state_ref[s0 + s]
            win_ref[s * per + CONV_HALO:(s + 1) * per, :] = u_ref[s * ls:(s + 1) * ls, :]
        conv([(s * per, ls, s * ls) for s in range(bt // ls)])

    y = _ln(y_ref[...] + bdw_ref[...], cg_ref[...], cb_ref[...])
    y = y * jax.nn.sigmoid(y)
    h = jnp.dot(y.astype(BF16), w2_ref[...], preferred_element_type=F32) + b2_ref[...]
    o_ref[...] = _ln(alpha * x_ref[...] + h, g_ref[...], b_ref[...])


def _conv_mix(u, state, x, dw, bdw, cg, cb, w2, b2, g, b, *, layer, bt, ls, lp, prompt_rows, alpha):
    n, d = u.shape
    tile, prev, vec = _mixer_specs(n, d, bt, CONV_HALO)
    win_rows = max(CONV_HALO + bt, (bt // ls) * (CONV_HALO + ls))
    return pl.pallas_call(
        functools.partial(_conv_kernel, bt=bt, ls=ls, tiles_per_stream=lp // bt,
                          prompt_tiles=prompt_rows // bt, alpha=alpha),
        grid=(n // bt,),
        in_specs=[tile, prev, pl.BlockSpec(state.shape, lambda i: (0, 0, 0)), tile,
                  pl.BlockSpec(dw.shape, lambda i: (0, 0)), vec, vec, vec,
                  pl.BlockSpec((None, d, d), lambda i: (layer, 0, 0)), vec, vec, vec],
        out_specs=tile,
        out_shape=jax.ShapeDtypeStruct((n, d), F32),
        scratch_shapes=[pltpu.VMEM((win_rows, d), F32), pltpu.VMEM((bt, d), F32)],
        compiler_params=_params(("arbitrary",)),
        name="conv_mix",
    )(u, u, state, x, dw, bdw, cg, cb, w2, b2, g, b)


def _pool_kernel(x_ref, halo_ref, state_ref, pw_ref, sc_ref, g_ref, b_ref, r_ref, *rest,
                 bt, ls, tiles_per_stream, prompt_tiles, n_experts, alpha, n_cast):
    cast_in, (o_ref, route_ref, *rest) = rest[:n_cast], rest[n_cast:]
    cast_out, (win_ref, m_ref) = rest[:n_cast], rest[n_cast:]
    d = o_ref.shape[-1]
    groups = pw_ref.shape[0]
    gd = d // groups
    i = pl.program_id(0)

    def pool(base, rows, out, pos):
        for gi in range(groups):
            win = 2 ** (gi + 1)
            cnt = jnp.minimum(pos + 1, win).astype(F32)
            for lo in range(gi * gd, (gi + 1) * gd, LANES):
                xw = win_ref[base:base + POOL_HALO + rows, lo:lo + LANES]
                s, shift = xw, 1
                while shift < win:
                    s = s + pltpu.roll(s, shift, axis=0)
                    shift *= 2
                m_ref[out:out + rows, lo:lo + LANES] = s[POOL_HALO:] / cnt - xw[POOL_HALO:]

    @pl.when(i < prompt_tiles)
    def _():
        t = i % tiles_per_stream
        win_ref[0:POOL_HALO, :] = jnp.where(t == 0, 0.0, halo_ref[...])
        win_ref[POOL_HALO:POOL_HALO + bt, :] = x_ref[...]
        for r0 in range(0, bt, POOL_PIECE):
            pool(r0, POOL_PIECE, r0, t * bt + r0 + lax.broadcasted_iota(jnp.int32, (POOL_PIECE, 1), 0))

    @pl.when(i >= prompt_tiles)
    def _():
        per = POOL_HALO + ls
        s0 = (i - prompt_tiles) * (bt // ls)
        pos = PAST_LEN + lax.broadcasted_iota(jnp.int32, (ls, 1), 0)
        for s in range(bt // ls):
            win_ref[s * per:s * per + POOL_HALO, :] = state_ref[s0 + s]
            win_ref[s * per + POOL_HALO:(s + 1) * per, :] = x_ref[s * ls:(s + 1) * ls, :]
            pool(s * per, ls, s * ls, pos)

    outs = [jnp.dot(m_ref[:, gi * gd:(gi + 1) * gd].astype(BF16), pw_ref[gi], preferred_element_type=F32)
            for gi in range(groups)]
    h = jnp.concatenate(outs, axis=-1) * sc_ref[...]
    x1 = _ln(alpha * x_ref[...] + h, g_ref[...], b_ref[...])
    o_ref[...] = x1

    def halves(v):
        hi = v.astype(BF16)
        return hi, (v - hi.astype(F32)).astype(BF16)
    xh, xl = halves(x1)
    rh, rl = halves(r_ref[...])
    logits = (jnp.dot(xh, rh, preferred_element_type=F32) + jnp.dot(xl, rh, preferred_element_type=F32)
              + jnp.dot(xh, rl, preferred_element_type=F32))
    lane = lax.broadcasted_iota(jnp.int32, logits.shape, 1).astype(F32)
    neg = -jnp.inf
    lg = jnp.where(lane < n_experts, logits, neg)
    v1 = jnp.max(lg, axis=-1, keepdims=True)
    i1 = jnp.min(jnp.where(lg == v1, lane, float(LANES)), axis=-1, keepdims=True)
    lg2 = jnp.where(lane == i1, neg, lg)
    v2 = jnp.max(lg2, axis=-1, keepdims=True)
    i2 = jnp.min(jnp.where(lg2 == v2, lane, float(LANES)), axis=-1, keepdims=True)
    e2 = jnp.exp(v2 - v1)
    den = 1.0 + e2
    route_ref[...] = jnp.where(lane == 0, i1, jnp.where(lane == 1, i2, jnp.where(
        lane == 2, 1.0 / den, jnp.where(lane == 3, e2 / den, 0.0))))
    _cast_blocks(cast_in, cast_out)


def _pool_mix(x, state, pw, sc, g, b, router, cast=(), *, layer, bt, ls, lp, prompt_rows, n_experts, alpha):
    n, d = x.shape
    tile, prev, vec = _mixer_specs(n, d, bt, POOL_HALO)
    win_rows = max(POOL_HALO + bt, (bt // ls) * (POOL_HALO + ls))
    cast_in, cast_out, cast_shapes = _cast_io(cast, lambda i: i, n // bt)
    return pl.pallas_call(
        functools.partial(_pool_kernel, bt=bt, ls=ls, tiles_per_stream=lp // bt,
                          prompt_tiles=prompt_rows // bt, n_experts=n_experts, alpha=alpha, n_cast=len(cast)),
        grid=(n // bt,),
        in_specs=[tile, prev, pl.BlockSpec(state.shape, lambda i: (0, 0, 0)),
                  pl.BlockSpec((None,) + pw.shape[1:], lambda i: (layer, 0, 0, 0)), vec, vec, vec,
                  pl.BlockSpec(router.shape, lambda i: (0, 0)), *cast_in],
        out_specs=[tile, pl.BlockSpec((bt, LANES), lambda i: (i, 0)), *cast_out],
        out_shape=[jax.ShapeDtypeStruct((n, d), F32), jax.ShapeDtypeStruct((n, LANES), F32), *cast_shapes],
        scratch_shapes=[pltpu.VMEM((win_rows, d), F32), pltpu.VMEM((bt, d), F32)],
        compiler_params=_params(("arbitrary",)),
        name="pool_mix",
    )(x, x, state, pw, sc, g, b, router, *[c[0] for c in cast])


def _ffn_kernel(x_ref, w1_ref, w3_ref, w2_ref, g_ref, b_ref, *rest, alpha, n_cast):
    cast_in, (o_ref, *rest) = rest[:n_cast], rest[n_cast:]
    cast_out, (xb_ref,) = rest[:n_cast], rest[n_cast:]
    f = pl.program_id(1)

    @pl.when(f == 0)
    def _():
        xb_ref[...] = x_ref[...].astype(BF16)
        o_ref[...] = jnp.zeros_like(o_ref)

    xb = xb_ref[...]
    h1 = jnp.dot(xb, w1_ref[...], preferred_element_type=F32)
    h3 = jnp.dot(xb, w3_ref[...], preferred_element_type=F32)
    h = (h1 * jax.nn.sigmoid(h1) * h3).astype(BF16)
    o_ref[...] += jnp.dot(h, w2_ref[...], preferred_element_type=F32)
    _cast_blocks(cast_in, cast_out)

    @pl.when(f == pl.num_programs(1) - 1)
    def _():
        o_ref[...] = _ln(alpha * x_ref[...] + o_ref[...], g_ref[...], b_ref[...])


def _ffn(x, w1, w3, w2, g, b, cast=(), *, layer, bm, bf, alpha):
    n, d = x.shape
    ff = w1.shape[-1]
    nf = ff // bf
    vec = pl.BlockSpec((1, d), lambda i, f: (0, 0))
    cast_in, cast_out, cast_shapes = _cast_io(cast, lambda i, f: i * nf + f, (n // bm) * nf)
    return pl.pallas_call(
        functools.partial(_ffn_kernel, alpha=alpha, n_cast=len(cast)),
        grid=(n // bm, nf),
        in_specs=[
            pl.BlockSpec((bm, d), lambda i, f: (i, 0)),
            pl.BlockSpec((None, d, bf), lambda i, f: (layer, 0, f)),
            pl.BlockSpec((None, d, bf), lambda i, f: (layer, 0, f)),
            pl.BlockSpec((None, bf, d), lambda i, f: (layer, f, 0)),
            vec, vec, *cast_in,
        ],
        out_specs=[pl.BlockSpec((bm, d), lambda i, f: (i, 0)), *cast_out],
        out_shape=[jax.ShapeDtypeStruct((n, d), F32), *cast_shapes],
        scratch_shapes=[pltpu.VMEM((bm, d), BF16)],
        compiler_params=_params(("arbitrary", "arbitrary")),
        name="ffn_dense",
    )(x, w1, w3, w2, g, b, *[c[0] for c in cast])


def _moe_kernel(vt_ref, ve_ref, vlo_ref, vhi_ref, nv_ref, src_ref, dst_ref,
                x_hbm, w1_ref, w3_ref, w2_ref, *rest, bm, n_tiles, chunk, n_cast):
    cast_in, (y_hbm, *rest) = rest[:n_cast], rest[n_cast:]
    cast_out, (xf_ref, xb_ref, acc_ref, yb_ref, gsem, ssem) = rest[:n_cast], rest[n_cast:]

    cast_blocks = functools.partial(_cast_blocks, cast_in, cast_out)

    v = pl.program_id(0)
    f = pl.program_id(1)
    nf = pl.num_programs(1)
    nv = nv_ref[0]
    valid = v < nv
    tile = vt_ref[v]
    lo = vlo_ref[v]
    hi = vhi_ref[v]
    first = lo == 0
    last = hi == bm

    def gather_row(t, r):
        tok = src_ref[t * bm + r]
        return pltpu.make_async_copy(x_hbm.at[pl.ds(tok, 1)], xf_ref.at[pl.ds(r, 1)], gsem)

    def scatter_row(t, r):
        row = dst_ref[t * bm + r]
        return pltpu.make_async_copy(yb_ref.at[pl.ds(r, 1)], y_hbm.at[pl.ds(row, 1)], ssem)

    def gather_tile_wait():
        pltpu.make_async_copy(x_hbm.at[pl.ds(0, bm)], xf_ref, gsem).wait()

    def scatter_tile_wait():
        pltpu.make_async_copy(yb_ref, y_hbm.at[pl.ds(0, bm)], ssem).wait()

    @pl.when(jnp.logical_and(valid, jnp.logical_and(first, f == 0)))
    def _():
        @pl.when(v == 0)
        def _():
            def issue(r, carry):
                gather_row(0, r).start()
                return carry
            lax.fori_loop(0, bm, issue, 0, unroll=8)
        gather_tile_wait()
        xb_ref[...] = xf_ref[...].astype(BF16)
        acc_ref[...] = jnp.zeros_like(acc_ref)

    do_gather = jnp.logical_and(last, tile + 1 < n_tiles)
    do_scatter = jnp.logical_and(first, tile > 0)

    @pl.when(valid)
    def _():
        for i in range(chunk):
            r = f * chunk + i
            in_tile = r < bm
            rc = jnp.minimum(r, bm - 1)

            @pl.when(jnp.logical_and(do_gather, in_tile))
            def _():
                gather_row(jnp.minimum(tile + 1, n_tiles - 1), rc).start()

            @pl.when(jnp.logical_and(do_scatter, in_tile))
            def _():
                scatter_row(jnp.maximum(tile - 1, 0), rc).start()

        xb = xb_ref[...]
        h1 = jnp.dot(xb, w1_ref[...], preferred_element_type=F32)
        h3 = jnp.dot(xb, w3_ref[...], preferred_element_type=F32)
        rid = lax.broadcasted_iota(jnp.int32, (bm, 1), 0)
        mine = jnp.logical_and(rid >= lo, rid < hi)
        h = jnp.where(mine, h1 * jax.nn.sigmoid(h1) * h3, 0.0).astype(BF16)
        acc_ref[...] += jnp.dot(h, w2_ref[...], preferred_element_type=F32)
        cast_blocks()

    @pl.when(jnp.logical_not(valid))
    def _():
        cast_blocks()

    @pl.when(jnp.logical_and(valid, f == nf - 1))
    def _():
        @pl.when(do_scatter)
        def _():
            scatter_tile_wait()

        @pl.when(last)
        def _():
            yb_ref[...] = acc_ref[...]

        @pl.when(v == nv - 1)
        def _():
            def issue(r, carry):
                scatter_row(tile, r).start()
                return carry
            lax.fori_loop(0, bm, issue, 0, unroll=8)
            scatter_tile_wait()


def _moe(x, plan, w1, w3, w2, cast=(), *, bm, bf):
    n, d = x.shape
    ff = w1.shape[-1]
    nf = ff // bf
    n_visits = plan[0].shape[0]
    n_tiles = TOP_K * n // bm
    chunk = -(-bm // (nf - 1))

    def fidx(v, f, nv):
        return jnp.where(v < nv[0], f, nf - 1)

    up = lambda v, f, vt, ve, vlo, vhi, nv, *_: (ve[v], 0, fidx(v, f, nv))
    down = lambda v, f, vt, ve, vlo, vhi, nv, *_: (ve[v], fidx(v, f, nv), 0)
    cast_in, cast_out, cast_shapes = _cast_io(cast, lambda v, f, *_: v * nf + f, n_visits * nf)
    grid_spec = pltpu.PrefetchScalarGridSpec(
        num_scalar_prefetch=7,
        grid=(n_visits, nf),
        in_specs=[
            pl.BlockSpec(memory_space=pl.ANY),
            pl.BlockSpec((None, d, bf), up),
            pl.BlockSpec((None, d, bf), up),
            pl.BlockSpec((None, bf, d), down),
            *cast_in,
        ],
        out_specs=[pl.BlockSpec(memory_space=pl.ANY), *cast_out],
        scratch_shapes=[
            pltpu.VMEM((bm, d), F32), pltpu.VMEM((bm, d), BF16), pltpu.VMEM((bm, d), F32),
            pltpu.VMEM((bm, d), F32), pltpu.SemaphoreType.DMA(()), pltpu.SemaphoreType.DMA(()),
        ],
    )
    return pl.pallas_call(
        functools.partial(_moe_kernel, bm=bm, n_tiles=n_tiles, chunk=chunk, n_cast=len(cast)),
        grid_spec=grid_spec,
        out_shape=[jax.ShapeDtypeStruct((TOP_K * n, d), F32), *cast_shapes],
        compiler_params=_params(("arbitrary", "arbitrary")),
        name="moe_experts",
    )(*plan, x, w1, w3, w2, *[c[0] for c in cast])


def _route_plan(route, *, n_experts, bm):
    n = route.shape[0]
    e_flat = route[:, :TOP_K].astype(jnp.int32).reshape(-1)
    a = e_flat.shape[0]
    n_tiles = a // bm
    n_visits = n_tiles + n_experts - 1
    asg = jnp.sort(e_flat * a + jnp.arange(a, dtype=jnp.int32)) % a
    src = asg // TOP_K
    dst = (asg % TOP_K) * n + src
    counts = jnp.sum((e_flat[:, None] == jnp.arange(n_experts, dtype=jnp.int32)[None, :]).astype(jnp.int32), axis=0)
    gend = jnp.cumsum(counts)
    cuts = jnp.sort(jnp.concatenate([jnp.arange(n_tiles, dtype=jnp.int32) * bm, gend]))
    lo, hi = cuts[:-1], cuts[1:]
    empty = hi <= lo
    order = jnp.argsort(empty, stable=True)
    nv = jnp.sum(jnp.logical_not(empty)).astype(jnp.int32)
    keep = order[jnp.minimum(jnp.arange(n_visits), nv - 1)]
    lo, hi = lo[keep], hi[keep]
    tile = lo // bm
    expert = jnp.sum((gend[None, :] <= lo[:, None]).astype(jnp.int32), axis=1)
    i32 = lambda t: t.astype(jnp.int32)
    return i32(tile), i32(expert), i32(lo - tile * bm), i32(hi - tile * bm), nv.reshape(1), i32(src), i32(dst)


def _combine_kernel(x_ref, y0_ref, y1_ref, route_ref, g_ref, b_ref, *o_refs, alpha, head_blocks):
    r = route_ref[...]
    f = r[:, 2:3] * y0_ref[...] + r[:, 3:4] * y1_ref[...]
    res = _ln(alpha * x_ref[...] + f, g_ref[...], b_ref[...])
    if head_blocks is None:
        o_refs[0][...] = res
    else:
        @pl.when(pl.program_id(0) < head_blocks)
        def _():
            o_refs[0][...] = res

        @pl.when(pl.program_id(0) >= head_blocks)
        def _():
            o_refs[1][...] = res


def _combine(x, y, route, g, b, *, bm, alpha, split_rows=None):
    n, d = x.shape
    nb = n // bm
    vec = pl.BlockSpec((1, d), lambda i: (0, 0))
    if split_rows is None:
        head_blocks = None
        out_specs = pl.BlockSpec((bm, d), lambda i: (i, 0))
        out_shape = jax.ShapeDtypeStruct((n, d), F32)
    else:
        assert split_rows % bm == 0
        head_blocks = split_rows // bm
        out_specs = [pl.BlockSpec((bm, d), lambda i: (jnp.minimum(i, head_blocks - 1), 0)),
                     pl.BlockSpec((bm, d), lambda i: (jnp.maximum(i - head_blocks, 0), 0))]
        out_shape = [jax.ShapeDtypeStruct((split_rows, d), F32), jax.ShapeDtypeStruct((n - split_rows, d), F32)]
    return pl.pallas_call(
        functools.partial(_combine_kernel, alpha=alpha, head_blocks=head_blocks),
        grid=(nb,),
        in_specs=[
            pl.BlockSpec((bm, d), lambda i: (i, 0)),
            pl.BlockSpec((bm, d), lambda i: (i, 0)),
            pl.BlockSpec((bm, d), lambda i: (i + nb, 0)),
            pl.BlockSpec((bm, LANES), lambda i: (i, 0)),
            vec, vec,
        ],
        out_specs=out_specs,
        out_shape=out_shape,
        compiler_params=_params(("arbitrary",)),
        name="moe_combine",
    )(x, y, y, route, g, b)


def _largest_divisor(n, cap, mult):
    return max(c for c in range(mult, cap + 1, mult) if n % c == 0)


def _tail_rows(buf, streams, length, first_row, keep):
    return jnp.stack([lax.slice_in_dim(buf, first_row + (s + 1) * length - keep, first_row + (s + 1) * length)
                      for s in range(streams)])


def kernel(x_prompt, x_sample, cache_conv, cache_pool, conv_w1, conv_b1, conv_dw, conv_bdw, conv_ln_g, conv_ln_b, conv_w2, conv_b2, pool_w, pool_scale, ffn_w1, ffn_w3, ffn_w2, moe_router, moe_w1, moe_w3, moe_w2, ln_g, ln_b):
    bp, lp, d = x_prompt.shape
    bs, ls, _ = x_sample.shape
    depth = ln_g.shape[0]
    alpha = (2.0 * depth) ** 0.25
    n_experts = moe_router.shape[-1]
    width = conv_dw.shape[1]
    pmax = cache_pool.shape[2]
    np_, ns = bp * lp, bs * ls
    n = np_ + ns

    bt = 256
    assert lp % bt == 0 and ns % bt == 0 and bt % ls == 0 and bt % CONV_HALO == 0
    assert bt % CONV_PIECE == 0 and bt % POOL_PIECE == 0
    assert width - 1 <= CONV_HALO and pmax <= POOL_HALO and ls >= width - 1 and ls >= pmax
    bm_tok = _largest_divisor(n, 640, 8)
    bm_comb = bt
    bm_moe, bf = 512, 512
    assert (TOP_K * n) % bm_moe == 0 and TOP_K * n * n_experts < 2 ** 31

    conv_w1, conv_w2, pool_w = conv_w1.astype(BF16), conv_w2.astype(BF16), pool_w.astype(BF16)
    ffn_w1, ffn_w3, ffn_w2 = ffn_w1.astype(BF16), ffn_w3.astype(BF16), ffn_w2.astype(BF16)
    n_moe, _, _, ff = moe_w1.shape
    up_rows, down_rows = n_experts * d, n_experts * ff
    moe_f32 = (moe_w1.reshape(n_moe * up_rows, ff), moe_w3.reshape(n_moe * up_rows, ff),
               moe_w2.reshape(n_moe * down_rows, d))
    down = moe_w2[0].astype(BF16) if n_moe else None

    def cast_jobs(arrays, rows, layer, n_steps):
        return tuple((w, layer * r, r, _cast_rows(r, n_steps)) for w, r in zip(arrays, rows))

    def as_experts(up1, up3, down):
        return up1.reshape(n_experts, d, ff), up3.reshape(n_experts, d, ff), down.reshape(n_experts, ff, d)

    x = jnp.concatenate([x_prompt.reshape(np_, d), x_sample.reshape(ns, d)], axis=0)
    row = lambda v: v.reshape(1, d)
    mix = dict(bt=bt, ls=ls, lp=lp, prompt_rows=np_, alpha=alpha)

    new_conv_p, new_conv_s, new_pool_p, new_pool_s = [], [], [], []
    for i in range(depth):
        j = i // 2
        if i % 2 == 0:
            u = _glu(x, conv_w1, conv_b1[j], layer=j, bm=bm_tok, bn=1024)
            state = jnp.pad(cache_conv[j], ((0, 0), (CONV_HALO - (width - 1), 0), (0, 0)))
            feeds_experts = i + 1 < depth
            up1_job = cast_jobs(moe_f32[:1], (up_rows,), j, (n // bm_tok) * (ff // bf)) if feeds_experts else ()
            x1 = _conv_mix(u, state, x, conv_dw[j], row(conv_bdw[j]), row(conv_ln_g[j]), row(conv_ln_b[j]),
                           conv_w2, row(conv_b2[j]), row(ln_g[i, 0]), row(ln_b[i, 0]), layer=j, **mix)
            new_conv_p.append(_tail_rows(u, bp, lp, 0, width - 1))
            new_conv_s.append(_tail_rows(u, bs, ls, np_, width - 1))
            x, *up1 = _ffn(x1, ffn_w1, ffn_w3, ffn_w2, row(ln_g[i, 1]), row(ln_b[i, 1]), up1_job,
                           layer=j, bm=bm_tok, bf=bf, alpha=alpha)
        else:
            router = jnp.pad(moe_router[j], ((0, 0), (0, LANES - n_experts)))
            state = jnp.pad(cache_pool[j], ((0, 0), (POOL_HALO - pmax, 0), (0, 0)))
            new_pool_p.append(_tail_rows(x, bp, lp, 0, pmax))
            new_pool_s.append(_tail_rows(x, bs, ls, np_, pmax))
            up3_job = cast_jobs(moe_f32[1:2], (up_rows,), j, n // bt)
            x1, route, *up3 = _pool_mix(x, state, pool_w, row(pool_scale[j]), row(ln_g[i, 0]),
                                        row(ln_b[i, 0]), router, up3_job, layer=j, n_experts=n_experts, **mix)
            w1, w3, w2 = as_experts(*up1, *up3, down)
            plan = _route_plan(route, n_experts=n_experts, bm=bm_moe)
            moe_steps = (TOP_K * n // bm_moe + n_experts - 1) * (ff // bf)
            down_job = cast_jobs(moe_f32[2:], (down_rows,), j + 1, moe_steps) if j + 1 < n_moe else ()
            y, *next_down = _moe(x1, plan, w1, w3, w2, down_job, bm=bm_moe, bf=bf)
            down = next_down[0] if next_down else None
            x = _combine(x1, y, route, row(ln_g[i, 1]), row(ln_b[i, 1]), bm=bm_comb, alpha=alpha,
                         split_rows=np_ if i == depth - 1 else None)

    y_prompt, y_sample = x if isinstance(x, (list, tuple)) else (x[:np_], x[np_:])
    return (y_prompt.reshape(bp, lp, d), y_sample.reshape(bs, ls, d), jnp.stack(new_conv_p),
            jnp.stack(new_pool_p), jnp.stack(new_conv_s), jnp.stack(new_pool_s))
```
